```python
import jax, jax.numpy as jnp
from jax import lax
import numpy as np

D_MODEL = 1024
BATCH = 8
SEQ = 2048
DEPTH = 2
DEC_BATCH = 16
DEC_SEQ = 2048
PAST_LEN = 128

SSD_EXPAND = 2
D_INNER = SSD_EXPAND * D_MODEL
SSD_HEAD_DIM = 64
SSD_HEADS = D_INNER // SSD_HEAD_DIM
SSD_GROUPS = 8
SSD_STATE = 128
SSD_CHUNK = 128
D_CONV = 5
GN = SSD_GROUPS * SSD_STATE
CONV_DIM = D_INNER + 2 * GN
ATTN_HEAD_DIM = 64
HEADS_PER_GROUP = 8
DILATION_GROUPS = ((128, 1), (512, 4), (2048, 16))
ATTN_HEADS = HEADS_PER_GROUP * len(DILATION_GROUPS)
ATTN_WIDTH = ATTN_HEADS * ATTN_HEAD_DIM
ATTN_OUT_WIDTH = HEADS_PER_GROUP * ATTN_HEAD_DIM
REL_BUCKETS = 32
REL_MAX_DISTANCE = 1024
PEER_HEADS = 8
PEER_N_KEYS = 128
PEER_EXPERTS = PEER_N_KEYS * PEER_N_KEYS
PEER_KEY_DIM = 256
PEER_HALF = PEER_KEY_DIM // 2
PEER_TOPK = 16
PEER_TOKEN_BLOCK = 128
RMS_EPS = 1e-6

OFF_XBC = D_INNER
OFF_DT = OFF_XBC + CONV_DIM
OFF_Q = OFF_DT + 2 * SSD_HEADS
OFF_K = OFF_Q + ATTN_WIDTH
OFF_V = OFF_K + ATTN_WIDTH
OFF_GATE = OFF_V + ATTN_WIDTH
IN_COLS = OFF_GATE + 2 * D_MODEL

kernel_name = 'hybrid_ssd_dilated_attn_peer_encoder'


def rms_norm(x, w):
    xf = x.astype(jnp.float32)
    y = xf * lax.rsqrt(jnp.mean(xf * xf, axis=-1, keepdims=True) + RMS_EPS)
    return (y * w.astype(jnp.float32)).astype(x.dtype)


def t5_bucket(rel):
    half = REL_BUCKETS // 2
    max_exact = half // 2
    n = np.abs(rel)
    large = max_exact + (np.log(np.maximum(n, 1) / max_exact) / np.log(REL_MAX_DISTANCE / max_exact) * (half - max_exact)).astype(np.int32)
    large = np.minimum(large, half - 1)
    return np.where(rel > 0, half, 0) + np.where(n < max_exact, n, large)


def ssd_chunked(x, dt, A, B, C):
    b, L, H, P = x.shape
    G, N = B.shape[2], B.shape[3]
    R = H // G
    Q = SSD_CHUNK
    c = L // Q
    xg = (x * dt[..., None]).reshape(b, c, Q, G, R, P)
    Acs = jnp.cumsum((dt * A).reshape(b, c, Q, G, R), axis=2)
    Bc = B.reshape(b, c, Q, G, N)
    Cc = C.reshape(b, c, Q, G, N)
    tril = np.tril(np.ones((Q, Q), dtype=bool))[:, :, None, None]
    seg = jnp.exp(jnp.where(tril, Acs[:, :, :, None] - Acs[:, :, None, :], -jnp.inf))
    CB = jnp.einsum('bclgn,bcsgn->bclsg', Cc, Bc)
    y_diag = jnp.einsum('bclsgr,bcsgrp->bclgrp', CB[..., None] * seg, xg)
    decay_states = jnp.exp(Acs[:, :, -1:] - Acs)
    states = jnp.einsum('bclgn,bclgr,bclgrp->bcgrpn', Bc, decay_states, xg)
    chunk_decay = jnp.exp(Acs[:, :, -1])

    def step(h, inp):
        st, dec = inp
        return h * dec[..., None, None] + st, h

    h0 = jnp.zeros((b, G, R, P, N), jnp.float32)
    _, prev = lax.scan(step, h0, (jnp.moveaxis(states, 1, 0), jnp.moveaxis(chunk_decay, 1, 0)))
    prev = jnp.moveaxis(prev, 0, 1)
    y_off = jnp.einsum('bclgn,bcgrpn,bclgr->bclgrp', Cc, prev, jnp.exp(Acs))
    return (y_diag + y_off).reshape(b, L, H, P)


def ssd_branch(z, xbc, dt_raw, conv_w, conv_b, A_log, dt_bias, D_skip, norm_w):
    b, L, _ = xbc.shape
    pad = D_CONV // 2
    xbc = lax.conv_general_dilated(xbc, conv_w[:, None, :], window_strides=(1,), padding=[(pad, pad)],
                                   dimension_numbers=('NWC', 'WIO', 'NWC'), feature_group_count=CONV_DIM) + conv_b
    xbc = jax.nn.silu(xbc).astype(jnp.float32)
    xs = xbc[..., :D_INNER].reshape(b, L, SSD_HEADS, SSD_HEAD_DIM)
    Bm = xbc[..., D_INNER:D_INNER + GN].reshape(b, L, SSD_GROUPS, SSD_STATE)
    Cm = xbc[..., D_INNER + GN:].reshape(b, L, SSD_GROUPS, SSD_STATE)
    dt = jax.nn.softplus(dt_raw.astype(jnp.float32).reshape(b, L, 2, SSD_HEADS) + dt_bias.astype(jnp.float32))
    A = -jnp.exp(A_log.astype(jnp.float32))
    y_f = ssd_chunked(xs, dt[:, :, 0], A[0], Bm, Cm)
    flip = lambda t: jnp.flip(t, axis=1)
    y_b = flip(ssd_chunked(flip(xs), flip(dt[:, :, 1]), A[1], flip(Bm), flip(Cm)))
    y = y_f + y_b + D_skip.astype(jnp.float32)[:, None] * xs
    y = y.reshape(b, L, D_INNER) * jax.nn.silu(z.astype(jnp.float32))
    return rms_norm(y, norm_w).astype(z.dtype)


def dilated_window_attention(q, k, v, bias_table, window, dilation):
    b, L, h, e = q.shape
    span = (window // 2) // dilation
    W = span
    M = L // dilation
    nb = -(-M // W)
    Mp = nb * W

    def to_sub(t):
        t = t.reshape(b, M, dilation, h, e).transpose(0, 2, 1, 3, 4)
        return jnp.pad(t, ((0, 0), (0, 0), (0, Mp - M), (0, 0), (0, 0)))

    def windows(t):
        t = jnp.pad(t, ((0, 0), (0, 0), (W, W), (0, 0), (0, 0))).reshape(b, dilation, nb + 2, W, h, e)
        return jnp.concatenate([t[:, :, :-2], t[:, :, 1:-1], t[:, :, 2:]], axis=3)

    qs = to_sub(q).reshape(b, dilation, nb, W, h, e)
    ks = windows(to_sub(k))
    vs = windows(to_sub(v))
    i = np.arange(W)[:, None]
    j = np.arange(3 * W)[None, :]
    rel_sub = j - W - i
    bias = jnp.transpose(bias_table[t5_bucket(rel_sub * dilation)], (2, 0, 1)).astype(jnp.float32)
    blk = np.arange(nb)[:, None, None]
    m_k = (blk - 1) * W + j[None]
    valid = (np.abs(rel_sub)[None] <= span) & (m_k >= 0) & (m_k < M)
    s = jnp.einsum('bdnqhe,bdnkhe->bdnhqk', qs, ks).astype(jnp.float32) * (e ** -0.5) + bias[None, None, None]
    s = jnp.where(valid[None, None, :, None], s, -1e30)
    m = jnp.max(s, axis=-1, keepdims=True)
    p = jnp.exp(s - m)
    den = jnp.sum(p, axis=-1, keepdims=True)
    o = jnp.einsum('bdnhqk,bdnkhe->bdnqhe', (p / den).astype(v.dtype), vs)
    lse = (m + jnp.log(den))[..., 0].transpose(0, 1, 2, 4, 3)
    o = o.reshape(b, dilation, Mp, h, e)[:, :, :M].transpose(0, 2, 1, 3, 4).reshape(b, L, h, e)
    lse = lse.reshape(b, dilation, Mp, h)[:, :, :M].transpose(0, 2, 1, 3).reshape(b, L, h)
    return o, lse


def attention_branch(q, k, v, rel_bias):
    b, L = q.shape[0], q.shape[1]
    q = q.reshape(b, L, ATTN_HEADS, ATTN_HEAD_DIM)
    k = k.reshape(b, L, ATTN_HEADS, ATTN_HEAD_DIM)
    v = v.reshape(b, L, ATTN_HEADS, ATTN_HEAD_DIM)
    outs, lses = [], []
    for gi, (win, dil) in enumerate(DILATION_GROUPS):
        hs = slice(gi * HEADS_PER_GROUP, (gi + 1) * HEADS_PER_GROUP)
        o, lse = dilated_window_attention(q[:, :, hs], k[:, :, hs], v[:, :, hs], rel_bias[:, hs], win, dil)
        outs.append(o)
        lses.append(lse)
    wts = jax.nn.softmax(jnp.stack(lses), axis=0)
    o = jnp.sum(wts[..., None] * jnp.stack(outs).astype(jnp.float32), axis=0)
    return o.reshape(b, L, ATTN_OUT_WIDTH).astype(q.dtype)


def peer_ffn(xn, w_query, sub_keys, expert_u, expert_v):
    b, L, D = xn.shape
    T = b * L
    xt = xn.reshape(T, D)
    q = (xt @ w_query).reshape(T, PEER_HEADS, 2, PEER_HALF)
    s = jnp.einsum('thsk,snk->thsn', q, sub_keys).astype(jnp.float32)
    sv, si = lax.top_k(s, PEER_TOPK)
    cand = (sv[:, :, 0, :, None] + sv[:, :, 1, None, :]).reshape(T, PEER_HEADS, PEER_TOPK * PEER_TOPK)
    cv, ci = lax.top_k(cand, PEER_TOPK)
    i1 = jnp.take_along_axis(si[:, :, 0], ci // PEER_TOPK, axis=-1)
    i2 = jnp.take_along_axis(si[:, :, 1], ci % PEER_TOPK, axis=-1)
    idx = i1 * PEER_N_KEYS + i2
    g = jax.nn.softmax(cv, axis=-1)
    nblk = T // PEER_TOKEN_BLOCK

    def block(args):
        xb, ib, gb = args
        hid = jax.nn.gelu(jnp.einsum('td,thkd->thk', xb, expert_u[ib]).astype(jnp.float32), approximate=False)
        return jnp.einsum('thk,thkd->td', (gb * hid).astype(xb.dtype), expert_v[ib])

    out = lax.map(block, (xt.reshape(nblk, PEER_TOKEN_BLOCK, D),
                          idx.reshape(nblk, PEER_TOKEN_BLOCK, PEER_HEADS, PEER_TOPK),
                          g.reshape(nblk, PEER_TOKEN_BLOCK, PEER_HEADS, PEER_TOPK)))
    return out.reshape(b, L, D)


def trunk(x, rel_bias, norm1_w, w_in, conv_w, conv_b, A_log, dt_bias, D_skip, ssd_norm_w,
          w_ssd_out, w_attn_out, w_out, norm2_w, w_query, sub_keys, expert_u, expert_v, final_norm_w):
    b, L, _ = x.shape
    for l in range(DEPTH):
        hn = rms_norm(x, norm1_w[l])
        proj = hn @ w_in[l]
        y_ssd = ssd_branch(proj[..., :OFF_XBC], proj[..., OFF_XBC:OFF_DT], proj[..., OFF_DT:OFF_Q],
                           conv_w[l], conv_b[l], A_log[l], dt_bias[l], D_skip[l], ssd_norm_w[l])
        y_att = attention_branch(proj[..., OFF_Q:OFF_K], proj[..., OFF_K:OFF_V], proj[..., OFF_V:OFF_GATE], rel_bias)
        gates = jax.nn.sigmoid(proj[..., OFF_GATE:].astype(jnp.float32)).reshape(b, L, 2, D_MODEL)
        merged = gates[:, :, 0] * (y_ssd @ w_ssd_out[l]) + gates[:, :, 1] * (y_att @ w_attn_out[l])
        x = x + merged.astype(x.dtype) @ w_out[l]
        x = x + peer_ffn(rms_norm(x, norm2_w[l]), w_query[l], sub_keys[l], expert_u[l], expert_v[l])
    return rms_norm(x, final_norm_w)


def setup_inputs(seed: int = 0) -> dict:
    key = jax.random.key(seed)
    ks = jax.random.split(key, 24)
    f32 = jnp.float32
    nrm = lambda k, shape, scale: jax.random.normal(k, shape, f32) * scale
    dt0 = jnp.exp(jax.random.uniform(ks[7], (DEPTH, 2, SSD_HEADS), f32, np.log(1e-3), np.log(1e-1)))
    return {
        'x_prompt': nrm(ks[0], (BATCH, SEQ, D_MODEL), 1.0),
        'x_sample': nrm(ks[1], (DEC_BATCH, DEC_SEQ, D_MODEL), 1.0),
        'rel_bias': nrm(ks[2], (REL_BUCKETS, ATTN_HEADS), 0.5),
        'norm1_w': 1.0 + nrm(ks[3], (DEPTH, D_MODEL), 0.05),
        'w_in': nrm(ks[4], (DEPTH, D_MODEL, IN_COLS), D_MODEL ** -0.5),
        'conv_w': nrm(ks[5], (DEPTH, D_CONV, CONV_DIM), D_CONV ** -0.5),
        'conv_b': nrm(ks[6], (DEPTH, CONV_DIM), 0.01),
        'A_log': jnp.log(jax.random.uniform(ks[8], (DEPTH, 2, SSD_HEADS), f32, 1.0, 16.0)),
        'dt_bias': dt0 + jnp.log(-jnp.expm1(-dt0)),
        'D_skip': 1.0 + nrm(ks[9], (DEPTH, SSD_HEADS), 0.1),
        'ssd_norm_w': 1.0 + nrm(ks[10], (DEPTH, D_INNER), 0.05),
        'w_ssd_out': nrm(ks[11], (DEPTH, D_INNER, D_MODEL), D_INNER ** -0.5),
        'w_attn_out': nrm(ks[12], (DEPTH, ATTN_OUT_WIDTH, D_MODEL), ATTN_OUT_WIDTH ** -0.5),
        'w_out': nrm(ks[13], (DEPTH, D_MODEL, D_MODEL), D_MODEL ** -0.5),
        'norm2_w': 1.0 + nrm(ks[14], (DEPTH, D_MODEL), 0.05),
        'w_query': nrm(ks[15], (DEPTH, D_MODEL, PEER_HEADS * PEER_KEY_DIM), D_MODEL ** -0.5),
        'sub_keys': nrm(ks[16], (DEPTH, 2, PEER_N_KEYS, PEER_HALF), PEER_HALF ** -0.5),
        'expert_u': nrm(ks[17], (DEPTH, PEER_EXPERTS, D_MODEL), D_MODEL ** -0.5),
        'expert_v': nrm(ks[18], (DEPTH, PEER_EXPERTS, D_MODEL), PEER_HEADS ** -0.5),
        'final_norm_w': 1.0 + nrm(ks[19], (D_MODEL,), 0.05),
    }


def reference(x_prompt, x_sample, rel_bias, norm1_w, w_in, conv_w, conv_b, A_log, dt_bias, D_skip,
              ssd_norm_w, w_ssd_out, w_attn_out, w_out, norm2_w, w_query, sub_keys, expert_u, expert_v,
              final_norm_w):
    y_prompt = trunk(x_prompt, rel_bias, norm1_w, w_in, conv_w, conv_b, A_log, dt_bias, D_skip, ssd_norm_w,
                     w_ssd_out, w_attn_out, w_out, norm2_w, w_query, sub_keys, expert_u, expert_v, final_norm_w)
    y_sample = trunk(x_sample, rel_bias, norm1_w, w_in, conv_w, conv_b, A_log, dt_bias, D_skip, ssd_norm_w,
                     w_ssd_out, w_attn_out, w_out, norm2_w, w_query, sub_keys, expert_u, expert_v, final_norm_w)
    return (y_prompt, y_sample)
```

```python
import functools

import numpy as np
import jax
import jax.numpy as jnp
from jax import lax
from jax.experimental import pallas as pl
from jax.experimental.pallas import tpu as pltpu

F32 = jnp.float32
BF16 = jnp.bfloat16
HIGHEST = lax.Precision.HIGHEST

D_MODEL = 1024
DEPTH = 2
D_INNER = 2048
SSD_HEADS = 32
SSD_HEAD_DIM = 64
SSD_GROUPS = 8
SSD_STATE = 128
CHUNK = 128
D_CONV = 5
GN = SSD_GROUPS * SSD_STATE
CONV_DIM = D_INNER + 2 * GN
ATTN_HEAD_DIM = 64
HEADS_PER_GROUP = 8
DILATION_GROUPS = ((128, 1), (512, 4), (2048, 16))
ATTN_WIDTH = 1536
ATTN_OUT_WIDTH = 512
REL_BUCKETS = 32
REL_MAX_DISTANCE = 1024
PEER_HEADS = 8
PEER_N_KEYS = 128
PEER_EXPERTS = PEER_N_KEYS * PEER_N_KEYS
PEER_HALF = 128
PEER_TOPK = 16
RMS_EPS = 1e-6

OFF_XBC = D_INNER
OFF_DT = OFF_XBC + CONV_DIM
OFF_Q = OFF_DT + 2 * SSD_HEADS

OFF_K = OFF_Q + ATTN_WIDTH
OFF_V = OFF_K + ATTN_WIDTH
OFF_GATE = OFF_V + ATTN_WIDTH

PC_Z = 0
PC_GATE = D_INNER
PC_XBC = PC_GATE + 2 * D_MODEL
PC_Q = PC_XBC + CONV_DIM
PC_K = PC_Q + ATTN_WIDTH
PC_V = PC_K + ATTN_WIDTH
PROJ_COLS = PC_V + ATTN_WIDTH
DT_REP = 16
DTX_COLS = SSD_GROUPS * 128

ATT_SPAN = 64
ATT_TQ = 128
ATT_TK = 256
NEG = -1e30

VMEM_LIMIT = 56 * 1024 * 1024


def _cparams(n_axes):
    return pltpu.CompilerParams(
        dimension_semantics=("arbitrary",) * n_axes, vmem_limit_bytes=VMEM_LIMIT
    )


def _inproj_kernel(x_ref, nw_ref, w_ref, wdt_ref, o_ref, dt_ref, xn_ref):
    @pl.when(pl.program_id(1) == 0)
    def _():
        x = x_ref[...]
        ms = jnp.mean(x * x, axis=-1, keepdims=True)
        xn = ((x * lax.rsqrt(ms + RMS_EPS)) * nw_ref[...]).astype(BF16)
        xn_ref[...] = xn
        dt_ref[...] = jnp.dot(xn, wdt_ref[...], preferred_element_type=F32)

    o_ref[...] = jnp.dot(xn_ref[...], w_ref[...], preferred_element_type=F32).astype(BF16)


def in_proj(x2d, norm_w, w_main, w_dtx, tm=512, tn=1280):
    T = x2d.shape[0]
    nc = w_main.shape[1]
    return pl.pallas_call(
        _inproj_kernel,
        grid=(T // tm, nc // tn),
        in_specs=[
            pl.BlockSpec((tm, D_MODEL), lambda i, j: (i, 0)),
            pl.BlockSpec((1, D_MODEL), lambda i, j: (0, 0)),
            pl.BlockSpec((D_MODEL, tn), lambda i, j: (0, j)),
            pl.BlockSpec((D_MODEL, DTX_COLS), lambda i, j: (0, 0)),
        ],
        out_specs=[
            pl.BlockSpec((tm, tn), lambda i, j: (i, j)),
            pl.BlockSpec((tm, DTX_COLS), lambda i, j: (i, 0)),
        ],
        out_shape=[
            jax.ShapeDtypeStruct((T, nc), BF16),
            jax.ShapeDtypeStruct((T, DTX_COLS), F32),
        ],
        scratch_shapes=[pltpu.VMEM((tm, D_MODEL), BF16)],
        compiler_params=_cparams(2),
        name="in_proj",
    )(x2d, norm_w, w_main, w_dtx)


def _lane_pair(col0, col1, lane_lt64):
    return jnp.where(lane_lt64, col0, col1)


def _ssd_kernel(x_ref, b_ref, c_ref, dt_ref, cwx_ref, cwb_ref, cwc_ref, cbx_ref, cbb_ref, cbc_ref,
                dtb_ref, alog_ref, dsk_ref, y_ref,
                pad_ref, xs_ref, bt_ref, cm_ref, yacc_ref, st_ref):
    L = x_ref.shape[1]
    nch = L // CHUNK
    cw = jnp.concatenate([cwx_ref[...], cwb_ref[...], cwc_ref[...]], axis=1)
    cb = jnp.concatenate([cbx_ref[...], cbb_ref[...], cbc_ref[...]], axis=1)
    dsk = dsk_ref[0]

    zeros8 = jnp.zeros((8, 512), F32)
    pad_ref[0:8, :] = zeros8
    pad_ref[L + 8:L + 16, :] = zeros8

    def copy_body(c, carry):
        r = pl.multiple_of(c * CHUNK, CHUNK)
        dst = pl.ds(r + 8, CHUNK)
        pad_ref[dst, 0:256] = x_ref[0, pl.ds(r, CHUNK), :].astype(F32)
        pad_ref[dst, 256:384] = b_ref[0, pl.ds(r, CHUNK), :].astype(F32)
        pad_ref[dst, 384:512] = c_ref[0, pl.ds(r, CHUNK), :].astype(F32)
        return carry

    lax.fori_loop(0, nch, copy_body, 0)

    def conv_body(c, carry):
        r = pl.multiple_of(c * CHUNK, CHUNK)
        win = pad_ref[pl.ds(r, CHUNK + 16), :]
        acc = None
        for w in range(D_CONV):
            sh = (2 - w) % (CHUNK + 16)
            rolled = win if sh == 0 else pltpu.roll(win, sh, 0)
            term = rolled[8:8 + CHUNK, :] * cw[w:w + 1, :]
            acc = term if acc is None else acc + term
        acc = acc + cb
        act = acc * jax.nn.sigmoid(acc)
        xs = act[:, 0:256]
        xs_ref[pl.ds(r, CHUNK), :] = xs
        yacc_ref[pl.ds(r, CHUNK), :] = xs * dsk
        bt_ref[c] = act[:, 256:384].T.astype(BF16)
        cm_ref[pl.ds(r, CHUNK), :] = act[:, 384:512].astype(BF16)
        return carry

    lax.fori_loop(0, nch, conv_body, 0)

    st_ref[...] = jnp.zeros_like(st_ref)

    row = lax.broadcasted_iota(jnp.int32, (CHUNK, CHUNK), 0)
    col = lax.broadcasted_iota(jnp.int32, (CHUNK, CHUNK), 1)
    lane_lt64 = col < 64
    lane_lt64_row = lax.broadcasted_iota(jnp.int32, (1, CHUNK), 1) < 64
    a_neg = -jnp.exp(alog_ref[0])
    dtb = dtb_ref[0]

    def one_pass(d, c):
        r = pl.multiple_of(c * CHUNK, CHUNK)
        rows = pl.ds(r, CHUNK)
        dt = jax.nn.softplus(dt_ref[0, rows, :] + dtb)
        a = dt * a_neg
        if d == 0:
            mask = row >= col
            cs = jnp.dot(mask.astype(F32), a, precision=HIGHEST, preferred_element_type=F32)
            tot = cs[CHUNK - 1:CHUNK, :]
        else:
            mask = row <= col
            cs = jnp.dot(mask.astype(F32), a, precision=HIGHEST, preferred_element_type=F32)
            tot = cs[0:1, :]
        ein = jnp.exp(cs)
        dd = jnp.exp(tot - cs) * dt
        cdec = jnp.exp(tot)
        cs_t = cs.T
        cmat = cm_ref[rows, :]
        btc = bt_ref[c]
        cbm = jnp.dot(cmat, btc, preferred_element_type=F32)
        for p in range(2):
            k0 = 64 * d + DT_REP * 2 * p
            k1 = k0 + DT_REP
            x2 = xs_ref[rows, 128 * p:128 * p + 128]
            dt2 = _lane_pair(dt[:, k0:k0 + 1], dt[:, k1:k1 + 1], lane_lt64)
            dd2 = _lane_pair(dd[:, k0:k0 + 1], dd[:, k1:k1 + 1], lane_lt64)
            ein2 = _lane_pair(ein[:, k0:k0 + 1], ein[:, k1:k1 + 1], lane_lt64)
            cd2 = _lane_pair(cdec[:, k0:k0 + 1], cdec[:, k1:k1 + 1], lane_lt64_row)
            xg2 = (x2 * dt2).astype(BF16)
            xdec2 = (x2 * dd2).astype(BF16)
            ys = []
            for k in (k0, k1):
                diff = cs[:, k:k + 1] - cs_t[k:k + 1, :]
                seg = jnp.where(mask, jnp.exp(diff), 0.0)
                g = (cbm * seg).astype(BF16)
                ys.append(jnp.dot(g, xg2, preferred_element_type=F32))
            ydiag = jnp.where(lane_lt64, ys[0], ys[1])
            st = st_ref[d, p]
            yoff = jnp.dot(cmat, st.astype(BF16), preferred_element_type=F32) * ein2
            yacc_ref[rows, 128 * p:128 * p + 128] += ydiag + yoff
            snew = jnp.dot(btc, xdec2, preferred_element_type=F32)
            st_ref[d, p] = st * cd2 + snew

    def main_body(i, carry):
        one_pass(0, i)
        one_pass(1, nch - 1 - i)
        return carry

    lax.fori_loop(0, nch, main_body, 0)

    def out_body(c, carry):
        rows = pl.ds(pl.multiple_of(c * CHUNK, CHUNK), CHUNK)
        y_ref[0, rows, :] = yacc_ref[rows, :].astype(BF16)
        return carry

    lax.fori_loop(0, nch, out_body, 0)


def ssd_mixer(proj3, dtx3, conv_w, conv_b, dtb_g, alog_g, dsk_g):
    B, L, _ = proj3.shape
    xb = PC_XBC // 256
    bb = (PC_XBC + D_INNER) // 128
    cbk = (PC_XBC + D_INNER + GN) // 128
    return pl.pallas_call(
        _ssd_kernel,
        grid=(B, SSD_GROUPS),
        in_specs=[
            pl.BlockSpec((1, L, 256), lambda b, g: (b, 0, xb + g)),
            pl.BlockSpec((1, L, 128), lambda b, g: (b, 0, bb + g)),
            pl.BlockSpec((1, L, 128), lambda b, g: (b, 0, cbk + g)),
            pl.BlockSpec((1, L, 128), lambda b, g: (b, 0, g)),
            pl.BlockSpec((D_CONV, 256), lambda b, g: (0, g)),
            pl.BlockSpec((D_CONV, 128), lambda b, g: (0, D_INNER // 128 + g)),
            pl.BlockSpec((D_CONV, 128), lambda b, g: (0, (D_INNER + GN) // 128 + g)),
            pl.BlockSpec((1, 256), lambda b, g: (0, g)),
            pl.BlockSpec((1, 128), lambda b, g: (0, D_INNER // 128 + g)),
            pl.BlockSpec((1, 128), lambda b, g: (0, (D_INNER + GN) // 128 + g)),
            pl.BlockSpec((1, 1, 128), lambda b, g: (g, 0, 0)),
            pl.BlockSpec((1, 1, 128), lambda b, g: (g, 0, 0)),
            pl.BlockSpec((1, 1, 256), lambda b, g: (g, 0, 0)),
        ],
        out_specs=pl.BlockSpec((1, L, 256), lambda b, g: (b, 0, g)),
        out_shape=jax.ShapeDtypeStruct((B, L, D_INNER), BF16),
        scratch_shapes=[
            pltpu.VMEM((L + 16, 512), F32),
            pltpu.VMEM((L, 256), F32),
            pltpu.VMEM((L // CHUNK, SSD_STATE, CHUNK), BF16),
            pltpu.VMEM((L, SSD_STATE), BF16),
            pltpu.VMEM((L, 256), F32),
            pltpu.VMEM((2, 2, SSD_STATE, 128), F32),
        ],
        compiler_params=_cparams(2),
        name="ssd_mixer",
    )(proj3, proj3, proj3, dtx3, conv_w, conv_w, conv_w, conv_b, conv_b, conv_b, dtb_g, alog_g, dsk_g)


def _attn_kernel(q_ref, k_ref, v_ref, bias_ref, o_ref, lse_ref, kp_ref, vp_ref):
    M = q_ref.shape[1]
    nt = M // ATT_TQ
    width = q_ref.shape[2]
    zpad = jnp.zeros((ATT_SPAN, width), BF16)
    kp_ref[0:ATT_SPAN, :] = zpad
    kp_ref[M + ATT_SPAN:M + 2 * ATT_SPAN, :] = zpad
    vp_ref[0:ATT_SPAN, :] = zpad
    vp_ref[M + ATT_SPAN:M + 2 * ATT_SPAN, :] = zpad

    def copy_body(t, carry):
        r = pl.multiple_of(t * ATT_TQ, ATT_TQ)
        kp_ref[pl.ds(r + ATT_SPAN, ATT_TQ), :] = k_ref[0, pl.ds(r, ATT_TQ), :]
        vp_ref[pl.ds(r + ATT_SPAN, ATT_TQ), :] = v_ref[0, pl.ds(r, ATT_TQ), :]
        return carry

    lax.fori_loop(0, nt, copy_body, 0)

    lane = lax.broadcasted_iota(jnp.int32, (ATT_TQ, 128), 1)
    lane_lt64 = lane < 64
    jj = lax.broadcasted_iota(jnp.int32, (ATT_TQ, ATT_TK), 1)

    def tile_body(t, carry):
        r = pl.multiple_of(t * ATT_TQ, ATT_TQ)
        rows = pl.ds(r, ATT_TQ)
        q = q_ref[0, rows, :] * jnp.asarray(ATTN_HEAD_DIM ** -0.5, BF16)
        kw = kp_ref[pl.ds(r, ATT_TK), :]
        vw = vp_ref[pl.ds(r, ATT_TK), :]
        in_seq = (jj >= ATT_SPAN - r) & (jj < M + ATT_SPAN - r)
        lse_tile = jnp.zeros((ATT_TQ, 128), F32)
        outs = []
        for p in range(HEADS_PER_GROUP // 2):
            cols = slice(128 * p, 128 * p + 128)
            q2, k2, v2 = q[:, cols], kw[:, cols], vw[:, cols]
            pv, dens = [], []
            for hh in range(2):
                h = 2 * p + hh
                qm = jnp.where(lane_lt64 if hh == 0 else jnp.logical_not(lane_lt64), q2, jnp.zeros_like(q2))
                s = lax.dot_general(qm, k2, (((1,), (1,)), ((), ())), preferred_element_type=F32)
                s = jnp.where(in_seq, s + bias_ref[h], NEG)
                m = jnp.max(s, axis=-1, keepdims=True)
                pexp = jnp.exp(s - m)
                den = jnp.sum(pexp, axis=-1, keepdims=True)
                pv.append(jnp.dot(pexp.astype(BF16), v2, preferred_element_type=F32))
                dens.append(den)
                lse_tile = jnp.where(lane == h, m + jnp.log(den), lse_tile)
            o2 = jnp.where(lane_lt64, pv[0], pv[1]) / jnp.where(lane_lt64, dens[0], dens[1])
            outs.append(o2.astype(BF16))
        o_ref[0, rows, :] = jnp.concatenate(outs, axis=1)
        lse_ref[0, rows, :] = lse_tile
        return carry

    lax.fori_loop(0, nt, tile_body, 0)


def dilated_attention(proj3, bias_g, gi, dil):
    B, L, _ = proj3.shape
    M = L // dil
    width = HEADS_PER_GROUP * ATTN_HEAD_DIM
    nblk = PROJ_COLS // width
    pv = proj3.reshape(B, M, dil * PROJ_COLS)
    qb, kb, vb = PC_Q // width + gi, PC_K // width + gi, PC_V // width + gi
    o, lse = pl.pallas_call(
        _attn_kernel,
        grid=(B, dil),
        in_specs=[
            pl.BlockSpec((1, M, width), lambda b, r: (b, 0, r * nblk + qb)),
            pl.BlockSpec((1, M, width), lambda b, r: (b, 0, r * nblk + kb)),
            pl.BlockSpec((1, M, width), lambda b, r: (b, 0, r * nblk + vb)),
            pl.BlockSpec((HEADS_PER_GROUP, ATT_TQ, ATT_TK), lambda b, r: (0, 0, 0)),
        ],
        out_specs=[
            pl.BlockSpec((1, M, width), lambda b, r: (b, 0, r)),
            pl.BlockSpec((1, M, 128), lambda b, r: (b, 0, r)),
        ],
        out_shape=[
            jax.ShapeDtypeStruct((B, M, dil * width), BF16),
            jax.ShapeDtypeStruct((B, M, dil * 128), F32),
        ],
        scratch_shapes=[
            pltpu.VMEM((M + 2 * ATT_SPAN, width), BF16),
            pltpu.VMEM((M + 2 * ATT_SPAN, width), BF16),
        ],
        compiler_params=_cparams(2),
        name=f"dilated_attention_d{dil}",
    )(pv, pv, pv, bias_g)
    return o.reshape(B, L, width), lse.reshape(B, L, 128)


def _mix_kernel(x_ref, y_ref, z_ref, g_ref, o0_ref, o1_ref, o2_ref, l0_ref, l1_ref, l2_ref,
                nw_ref, wssd_ref, watt_ref, wout_ref, out_ref):
    z = z_ref[...].astype(F32)
    yz = y_ref[...].astype(F32) * (z * jax.nn.sigmoid(z))
    ms = jnp.mean(yz * yz, axis=-1, keepdims=True)
    yn = ((yz * lax.rsqrt(ms + RMS_EPS)) * nw_ref[...]).astype(BF16)
    pa = jnp.dot(yn, wssd_ref[...], preferred_element_type=F32)

    ls = [l0_ref[...], l1_ref[...], l2_ref[...]]
    mx = jnp.maximum(jnp.maximum(ls[0], ls[1]), ls[2])
    es = [jnp.exp(l - mx) for l in ls]
    den = es[0] + es[1] + es[2]
    erow = lax.broadcasted_iota(jnp.int32, (128, ATTN_OUT_WIDTH), 0)
    ecol = lax.broadcasted_iota(jnp.int32, (128, ATTN_OUT_WIDTH), 1)
    expand = (erow == ecol // ATTN_HEAD_DIM).astype(F32)
    yatt = None
    for e, o_ref in zip(es, (o0_ref, o1_ref, o2_ref)):
        wexp = jnp.dot(e / den, expand, precision=HIGHEST, preferred_element_type=F32)
        term = wexp * o_ref[...].astype(F32)
        yatt = term if yatt is None else yatt + term
    pb = jnp.dot(yatt.astype(BF16), watt_ref[...], preferred_element_type=F32)

    gates = jax.nn.sigmoid(g_ref[...].astype(F32))
    merged = gates[:, :D_MODEL] * pa + gates[:, D_MODEL:] * pb
    out_ref[...] = x_ref[...] + jnp.dot(merged.astype(BF16), wout_ref[...], preferred_element_type=F32)


def mix_project(x2d, yssd, proj, outs, lses, norm_w, w_ssd, w_att, w_out, tm=512):
    T = x2d.shape[0]
    row = lambda i: (i, 0)
    const = lambda i: (0, 0)
    return pl.pallas_call(
        _mix_kernel,
        grid=(T // tm,),
        in_specs=[
            pl.BlockSpec((tm, D_MODEL), row),
            pl.BlockSpec((tm, D_INNER), row),
            pl.BlockSpec((tm, D_INNER), lambda i: (i, PC_Z // D_INNER)),
            pl.BlockSpec((tm, 2 * D_MODEL), lambda i: (i, PC_GATE // (2 * D_MODEL))),
            pl.BlockSpec((tm, ATTN_OUT_WIDTH), row),
            pl.BlockSpec((tm, ATTN_OUT_WIDTH), row),
            pl.BlockSpec((tm, ATTN_OUT_WIDTH), row),
            pl.BlockSpec((tm, 128), row),
            pl.BlockSpec((tm, 128), row),
            pl.BlockSpec((tm, 128), row),
            pl.BlockSpec((1, D_INNER), const),
            pl.BlockSpec((D_INNER, D_MODEL), const),
            pl.BlockSpec((ATTN_OUT_WIDTH, D_MODEL), const),
            pl.BlockSpec((D_MODEL, D_MODEL), const),
        ],
        out_specs=pl.BlockSpec((tm, D_MODEL), row),
        out_shape=jax.ShapeDtypeStruct((T, D_MODEL), F32),
        compiler_params=_cparams(1),
        name="mix_project",
    )(x2d, yssd, proj, proj, outs[0], outs[1], outs[2], lses[0], lses[1], lses[2], norm_w, w_ssd, w_att, w_out)


ROUTE_LANES = 128
N_CAND = 80


def _extract_top(vals, ids, n, big):
    out_v, out_i = [], []
    for _ in range(n):
        m = jnp.max(vals, axis=0, keepdims=True)
        pick = jnp.min(jnp.where(vals == m, ids, big), axis=0, keepdims=True)
        out_v.append(m)
        out_i.append(pick)
        vals = jnp.where(ids == pick, -jnp.inf, vals)
    return out_v, out_i


def _stack_rows(rows):
    n = len(rows)
    rid = lax.broadcasted_iota(jnp.int32, (n, rows[0].shape[1]), 0)
    out = jnp.broadcast_to(rows[0], rid.shape)
    for k in range(1, n):
        out = jnp.where(rid == k, rows[k], out)
    return out


def _peer_route_kernel(x_ref, nw_ref, wq_ref, keys_ref, xn_ref, idx_ref, g_ref, qt_ref, it_ref, gt_ref):
    tm = x_ref.shape[0]
    x = x_ref[...]
    ms = jnp.mean(x * x, axis=-1, keepdims=True)
    xn = ((x * lax.rsqrt(ms + RMS_EPS)) * nw_ref[...]).astype(BF16)
    xn_ref[...] = xn
    qt_ref[...] = lax.dot_general(wq_ref[...], xn, (((1,), (1,)), ((), ())),
                                  preferred_element_type=F32).astype(BF16)

    key_id = lax.broadcasted_iota(jnp.int32, (PEER_N_KEYS, ROUTE_LANES), 0).astype(F32)
    r80 = lax.broadcasted_iota(jnp.int32, (N_CAND, ROUTE_LANES), 0)
    rank_a = jnp.where(r80 < 16, 0, jnp.where(r80 < 72, 1 + lax.shift_right_logical(r80 - 16, 3), r80 - 64))
    rank_b = jnp.where(r80 < 16, r80, jnp.where(r80 < 72, (r80 - 16) & 7, 0))
    code = (rank_a * PEER_TOPK + rank_b).astype(F32)

    def route(h, lane0):
        lanes = slice(lane0, lane0 + ROUTE_LANES)
        sv, si = [], []
        for s in range(2):
            qhs = qt_ref[pl.ds(pl.multiple_of((2 * h + s) * PEER_HALF, PEER_HALF), PEER_HALF), lanes]
            sc = jnp.dot(keys_ref[s], qhs, preferred_element_type=F32)
            v, i = _extract_top(sc, key_id, PEER_TOPK, float(PEER_N_KEYS))
            sv.append(v)
            si.append(i)
        sv1 = _stack_rows(sv[1])
        si1 = _stack_rows(si[1])
        cand = [sv[0][0] + sv1]
        eid = [si[0][0] * PEER_N_KEYS + si1]
        for a in range(1, 8):
            cand.append(sv[0][a] + sv1[0:8])
            eid.append(si[0][a] * PEER_N_KEYS + si1[0:8])
        cand.append(_stack_rows(sv[0][8:16]) + sv1[0:1])
        eid.append(_stack_rows(si[0][8:16]) * PEER_N_KEYS + si1[0:1])
        cand = jnp.concatenate(cand, axis=0)
        eid = jnp.concatenate(eid, axis=0)
        cv, cc = _extract_top(cand, code, PEER_TOPK, 256.0)
        picked = [jnp.max(jnp.where(code == c_k, eid, -1.0), axis=0, keepdims=True) for c_k in cc]
        cvs = _stack_rows(cv)
        ex = jnp.exp(cvs - cvs[0:1])
        gate = ex / jnp.sum(ex, axis=0, keepdims=True)
        rows = pl.ds(pl.multiple_of(h * PEER_TOPK, PEER_TOPK), PEER_TOPK)
        it_ref[rows, lanes] = _stack_rows(picked)
        gt_ref[rows, lanes] = gate

    def body(h, carry):
        for lane0 in range(0, tm, ROUTE_LANES):
            route(h, lane0)
        return carry

    lax.fori_loop(0, PEER_HEADS, body, 0)
    idx_ref[...] = it_ref[...].T.astype(jnp.int32)
    g_ref[...] = gt_ref[...].T


def peer_route(x2d, norm_w, wq_t, keys, tm=256):
    T = x2d.shape[0]
    npick = PEER_HEADS * PEER_TOPK
    return pl.pallas_call(
        _peer_route_kernel,
        grid=(T // tm,),
        in_specs=[
            pl.BlockSpec((tm, D_MODEL), lambda i: (i, 0)),
            pl.BlockSpec((1, D_MODEL), lambda i: (0, 0)),
            pl.BlockSpec((2 * PEER_HEADS * PEER_HALF, D_MODEL), lambda i: (0, 0)),
            pl.BlockSpec((2, PEER_N_KEYS, PEER_HALF), lambda i: (0, 0, 0)),
        ],
        out_specs=[
            pl.BlockSpec((tm, D_MODEL), lambda i: (i, 0)),
            pl.BlockSpec((tm, npick), lambda i: (i, 0)),
            pl.BlockSpec((tm, npick), lambda i: (i, 0)),
        ],
        out_shape=[
            jax.ShapeDtypeStruct((T, D_MODEL), BF16),
            jax.ShapeDtypeStruct((T, npick), jnp.int32),
            jax.ShapeDtypeStruct((T, npick), F32),
        ],
        scratch_shapes=[
            pltpu.VMEM((2 * PEER_HEADS * PEER_HALF, tm), BF16),
            pltpu.VMEM((npick, tm), F32),
            pltpu.VMEM((npick, tm), F32),
        ],
        compiler_params=_cparams(1),
        name="peer_route",
    )(x2d, norm_w, wq_t, keys)


def _peer_gates_kernel(idx_ref, g_ref, gd_ref, g3_ref):
    tm = idx_ref.shape[0]
    rid = lax.broadcasted_iota(jnp.int32, (PEER_N_KEYS, 128), 0)

    def body(t, carry):
        e = idx_ref[pl.ds(t, 1), :]
        g = g_ref[pl.ds(t, 1), :]
        i1 = lax.shift_right_logical(e, 7)
        i2 = e & (PEER_N_KEYS - 1)
        a_t = jnp.where(rid == i1, g, 0.0).astype(BF16)
        b_t = jnp.where(rid == i2, 1.0, 0.0).astype(BF16)
        gd = lax.dot_general(a_t, b_t, (((1,), (1,)), ((), ())), preferred_element_type=F32)
        g3_ref[pl.ds(pl.multiple_of(t * PEER_N_KEYS, PEER_N_KEYS), PEER_N_KEYS), :] = gd
        return carry

    lax.fori_loop(0, tm, body, 0)
    for a in range(PEER_N_KEYS):
        gd_ref[:, a * PEER_N_KEYS:(a + 1) * PEER_N_KEYS] = g3_ref[pl.ds(a, tm, stride=PEER_N_KEYS), :].astype(BF16)


def peer_gates(idx, g, tm=128):
    T, npick = idx.shape
    return pl.pallas_call(
        _peer_gates_kernel,
        grid=(T // tm,),
        in_specs=[pl.BlockSpec((tm, npick), lambda i: (i, 0)), pl.BlockSpec((tm, npick), lambda i: (i, 0))],
        out_specs=pl.BlockSpec((tm, PEER_EXPERTS), lambda i: (i, 0)),
        out_shape=jax.ShapeDtypeStruct((T, PEER_EXPERTS), BF16),
        scratch_shapes=[pltpu.VMEM((tm * PEER_N_KEYS, PEER_N_KEYS), F32)],
        compiler_params=_cparams(1),
        name="peer_gates",
    )(idx, g)


def _peer_experts_kernel(x_ref, xn_ref, gd_ref, ut_ref, v_ref, fw_ref, out_ref, acc_ref, *, final_norm):
    j = pl.program_id(1)

    @pl.when(j == 0)
    def _():
        acc_ref[...] = jnp.zeros_like(acc_ref)

    z = jnp.dot(xn_ref[...], ut_ref[...], preferred_element_type=F32)
    hid = 0.5 * z * (1.0 + lax.erf(z * np.float32(np.sqrt(0.5))))
    w = (gd_ref[...].astype(F32) * hid).astype(BF16)
    acc_ref[...] += jnp.dot(w, v_ref[...], preferred_element_type=F32)

    @pl.when(j == pl.num_programs(1) - 1)
    def _():
        y = x_ref[...] + acc_ref[...]
        if final_norm:
            ms = jnp.mean(y * y, axis=-1, keepdims=True)
            y = (y * lax.rsqrt(ms + RMS_EPS)) * fw_ref[...]
        out_ref[...] = y


def peer_experts(x2d, xn, gd, u_t, v, final_w, final_norm, tm=512, te=512):
    T = x2d.shape[0]
    return pl.pallas_call(
        functools.partial(_peer_experts_kernel, final_norm=final_norm),
        grid=(T // tm, PEER_EXPERTS // te),
        in_specs=[
            pl.BlockSpec((tm, D_MODEL), lambda i, j: (i, 0)),
            pl.BlockSpec((tm, D_MODEL), lambda i, j: (i, 0)),
            pl.BlockSpec((tm, te), lambda i, j: (i, j)),
            pl.BlockSpec((D_MODEL, te), lambda i, j: (0, j)),
            pl.BlockSpec((te, D_MODEL), lambda i, j: (j, 0)),
            pl.BlockSpec((1, D_MODEL), lambda i, j: (0, 0)),
        ],
        out_specs=pl.BlockSpec((tm, D_MODEL), lambda i, j: (i, 0)),
        out_shape=jax.ShapeDtypeStruct((T, D_MODEL), F32),
        scratch_shapes=[pltpu.VMEM((tm, D_MODEL), F32)],
        compiler_params=_cparams(2),
        name="peer_experts_final" if final_norm else "peer_experts",
    )(x2d, xn, gd, u_t, v, final_w)


def _t5_bucket(rel):
    half = REL_BUCKETS // 2
    max_exact = half // 2
    n = np.abs(rel)
    large = max_exact + (np.log(np.maximum(n, 1) / max_exact) / np.log(REL_MAX_DISTANCE / max_exact)
                         * (half - max_exact)).astype(np.int32)
    large = np.minimum(large, half - 1)
    return np.where(rel > 0, half, 0) + np.where(n < max_exact, n, large)


def prep_attn_bias(rel_bias, gi, dil):
    i = np.arange(ATT_TQ)[:, None]
    j = np.arange(ATT_TK)[None, :]
    rel = j - ATT_SPAN - i
    bucket = _t5_bucket(rel * dil)
    table = rel_bias[bucket][:, :, gi * HEADS_PER_GROUP:(gi + 1) * HEADS_PER_GROUP]
    table = jnp.where((np.abs(rel) <= ATT_SPAN)[:, :, None], table.astype(F32), NEG)
    return jnp.transpose(table, (2, 0, 1))


def _dt_col_index():
    g = np.arange(SSD_GROUPS)[:, None, None, None]
    d = np.arange(2)[None, :, None, None]
    h = np.arange(4)[None, None, :, None]
    rep = np.zeros((1, 1, 1, DT_REP), np.int64)
    return (d * SSD_HEADS + 4 * g + h + rep).reshape(-1)


def prep_w_in(w_in_l):
    w_main = jnp.concatenate(
        [w_in_l[:, :OFF_XBC], w_in_l[:, OFF_GATE:], w_in_l[:, OFF_XBC:OFF_DT], w_in_l[:, OFF_Q:OFF_GATE]], axis=1
    ).astype(BF16)
    w_dtx = w_in_l[:, OFF_DT + _dt_col_index()].astype(BF16)
    return w_main, w_dtx


def prep_ssd_params(dt_bias_l, a_log_l, d_skip_l):
    idx = _dt_col_index()
    dtb_g = dt_bias_l.reshape(-1)[idx].reshape(SSD_GROUPS, 1, 128)
    alog_g = a_log_l.reshape(-1)[idx].reshape(SSD_GROUPS, 1, 128)
    dsk_g = jnp.repeat(d_skip_l, SSD_HEAD_DIM).reshape(SSD_GROUPS, 1, 256)
    return dtb_g, alog_g, dsk_g


def trunk_layer(x2d, B, L, biases, norm1_w, w_in, conv_w, conv_b, a_log, dt_bias, d_skip, ssd_norm_w,
                w_ssd_out, w_attn_out, w_out, norm2_w, w_query, sub_keys, expert_u, expert_v, final_w, final_norm):
    w_main, w_dtx = prep_w_in(w_in)
    proj, dtx = in_proj(x2d, norm1_w.reshape(1, -1), w_main, w_dtx)
    proj3 = proj.reshape(B, L, PROJ_COLS)
    yssd = ssd_mixer(proj3, dtx.reshape(B, L, DTX_COLS), conv_w, conv_b.reshape(1, -1),
                     *prep_ssd_params(dt_bias, a_log, d_skip))
    outs, lses = [], []
    for gi, (_, dil) in enumerate(DILATION_GROUPS):
        o, lse = dilated_attention(proj3, biases[gi], gi, dil)
        outs.append(o.reshape(B * L, ATTN_OUT_WIDTH))
        lses.append(lse.reshape(B * L, 128))
    x1 = mix_project(x2d, yssd.reshape(B * L, D_INNER), proj, outs, lses, ssd_norm_w.reshape(1, -1),
                     w_ssd_out.astype(BF16), w_attn_out.astype(BF16), w_out.astype(BF16))
    xn, idx, gate = peer_route(x1, norm2_w.reshape(1, -1), w_query.T.astype(BF16), sub_keys.astype(BF16))
    gd = peer_gates(idx, gate)
    return peer_experts(x1, xn, gd, expert_u.T.astype(BF16), expert_v.astype(BF16), final_w.reshape(1, -1), final_norm)


def kernel(x_prompt, x_sample, rel_bias, norm1_w, w_in, conv_w, conv_b, A_log, dt_bias, D_skip, ssd_norm_w,
           w_ssd_out, w_attn_out, w_out, norm2_w, w_query, sub_keys, expert_u, expert_v, final_norm_w):
    nb_prompt = x_prompt.shape[0]
    x = jnp.concatenate([x_prompt, x_sample], axis=0)
    B, L, _ = x.shape
    x2d = x.reshape(B * L, D_MODEL)
    biases = [prep_attn_bias(rel_bias, gi, dil) for gi, (_, dil) in enumerate(DILATION_GROUPS)]
    for l in range(DEPTH):
        x2d = trunk_layer(x2d, B, L, biases, norm1_w[l], w_in[l], conv_w[l], conv_b[l], A_log[l], dt_bias[l],
                          D_skip[l], ssd_norm_w[l], w_ssd_out[l], w_attn_out[l], w_out[l], norm2_w[l], w_query[l],
                          sub_keys[l], expert_u[l], expert_v[l], final_norm_w, l == DEPTH - 1)
    y = x2d.reshape(B, L, D_MODEL)
    return y[:nb_prompt], y[nb_prompt:]
```

```python
import functools

import numpy as np
import jax
import jax.numpy as jnp
from jax import lax
from jax.experimental import pallas as pl
from jax.experimental.pallas import tpu as pltpu

F32 = jnp.float32
BF16 = jnp.bfloat16
HIGHEST = lax.Precision.HIGHEST

D_MODEL = 1024
DEPTH = 2
D_INNER = 2048
SSD_HEADS = 32
SSD_HEAD_DIM = 64
SSD_GROUPS = 8
SSD_STATE = 128
CHUNK = 128
D_CONV = 5
GN = SSD_GROUPS * SSD_STATE
CONV_DIM = D_INNER + 2 * GN
ATTN_HEAD_DIM = 64
HEADS_PER_GROUP = 8
DILATION_GROUPS = ((128, 1), (512, 4), (2048, 16))
ATTN_WIDTH = 1536
ATTN_OUT_WIDTH = 512
REL_BUCKETS = 32
REL_MAX_DISTANCE = 1024
PEER_HEADS = 8
PEER_N_KEYS = 128
PEER_EXPERTS = PEER_N_KEYS * PEER_N_KEYS
PEER_HALF = 128
PEER_TOPK = 16
RMS_EPS = 1e-6

OFF_XBC = D_INNER
OFF_DT = OFF_XBC + CONV_DIM
OFF_Q = OFF_DT + 2 * SSD_HEADS

OFF_K = OFF_Q + ATTN_WIDTH
OFF_V = OFF_K + ATTN_WIDTH
OFF_GATE = OFF_V + ATTN_WIDTH

PC_Z = 0
PC_GATE = D_INNER
PC_XBC = PC_GATE + 2 * D_MODEL
PC_Q = PC_XBC + CONV_DIM
PC_K = PC_Q + ATTN_WIDTH
PC_V = PC_K + ATTN_WIDTH
PROJ_COLS = PC_V + ATTN_WIDTH
DT_REP = 16
DTX_COLS = SSD_GROUPS * 128

ATT_SPAN = 64
ATT_TQ = 128
ATT_TK = 256
NEG = -1e30

VMEM_LIMIT = 56 * 1024 * 1024


def _cparams(n_axes):
    return pltpu.CompilerParams(
        dimension_semantics=("arbitrary",) * n_axes, vmem_limit_bytes=VMEM_LIMIT
    )


def _inproj_kernel(x_ref, nw_ref, w_ref, wdt_ref, o_ref, dt_ref, xn_ref):
    @pl.when(pl.program_id(1) == 0)
    def _():
        x = x_ref[...]
        ms = jnp.mean(x * x, axis=-1, keepdims=True)
        xn = ((x * lax.rsqrt(ms + RMS_EPS)) * nw_ref[...]).astype(BF16)
        xn_ref[...] = xn
        dt_ref[...] = jnp.dot(xn, wdt_ref[...], preferred_element_type=F32)

    o_ref[...] = jnp.dot(xn_ref[...], w_ref[...], preferred_element_type=F32).astype(BF16)


def in_proj(x2d, norm_w, w_main, w_dtx, tm=1024, tn=1280):
    T = x2d.shape[0]
    nc = w_main.shape[1]
    return pl.pallas_call(
        _inproj_kernel,
        grid=(T // tm, nc // tn),
        in_specs=[
            pl.BlockSpec((tm, D_MODEL), lambda i, j: (i, 0)),
            pl.BlockSpec((1, D_MODEL), lambda i, j: (0, 0)),
            pl.BlockSpec((D_MODEL, tn), lambda i, j: (0, j)),
            pl.BlockSpec((D_MODEL, DTX_COLS), lambda i, j: (0, 0)),
        ],
        out_specs=[
            pl.BlockSpec((tm, tn), lambda i, j: (i, j)),
            pl.BlockSpec((tm, DTX_COLS), lambda i, j: (i, 0)),
        ],
        out_shape=[
            jax.ShapeDtypeStruct((T, nc), BF16),
            jax.ShapeDtypeStruct((T, DTX_COLS), F32),
        ],
        scratch_shapes=[pltpu.VMEM((tm, D_MODEL), BF16)],
        compiler_params=_cparams(2),
        name="in_proj",
    )(x2d, norm_w, w_main, w_dtx)


def _lane_pair(col0, col1, lane_lt64):
    return jnp.where(lane_lt64, col0, col1)


def _ssd_kernel(x_ref, b_ref, c_ref, dt_ref, cwx_ref, cwb_ref, cwc_ref, cbx_ref, cbb_ref, cbc_ref,
                dtb_ref, alog_ref, dsk_ref, y_ref,
                pad_ref, xs_ref, bt_ref, cm_ref, yacc_ref, st_ref):
    L = x_ref.shape[1]
    nch = L // CHUNK
    cw = jnp.concatenate([cwx_ref[...], cwb_ref[...], cwc_ref[...]], axis=1)
    cb = jnp.concatenate([cbx_ref[...], cbb_ref[...], cbc_ref[...]], axis=1)
    dsk = dsk_ref[0]

    zeros8 = jnp.zeros((8, 512), F32)
    pad_ref[0:8, :] = zeros8
    pad_ref[L + 8:L + 16, :] = zeros8

    def copy_body(c, carry):
        r = pl.multiple_of(c * CHUNK, CHUNK)
        dst = pl.ds(r + 8, CHUNK)
        pad_ref[dst, 0:256] = x_ref[0, pl.ds(r, CHUNK), :].astype(F32)
        pad_ref[dst, 256:384] = b_ref[0, pl.ds(r, CHUNK), :].astype(F32)
        pad_ref[dst, 384:512] = c_ref[0, pl.ds(r, CHUNK), :].astype(F32)
        return carry

    lax.fori_loop(0, nch, copy_body, 0)

    def conv_body(c, carry):
        r = pl.multiple_of(c * CHUNK, CHUNK)
        win = pad_ref[pl.ds(r, CHUNK + 16), :]
        acc = None
        for w in range(D_CONV):
            sh = (2 - w) % (CHUNK + 16)
            rolled = win if sh == 0 else pltpu.roll(win, sh, 0)
            term = rolled[8:8 + CHUNK, :] * cw[w:w + 1, :]
            acc = term if acc is None else acc + term
        acc = acc + cb
        act = acc * jax.nn.sigmoid(acc)
        xs = act[:, 0:256]
        xs_ref[pl.ds(r, CHUNK), :] = xs
        yacc_ref[pl.ds(r, CHUNK), :] = xs * dsk
        bt_ref[c] = act[:, 256:384].T.astype(BF16)
        cm_ref[pl.ds(r, CHUNK), :] = act[:, 384:512].astype(BF16)
        return carry

    lax.fori_loop(0, nch, conv_body, 0)

    st_ref[...] = jnp.zeros_like(st_ref)

    row = lax.broadcasted_iota(jnp.int32, (CHUNK, CHUNK), 0)
    col = lax.broadcasted_iota(jnp.int32, (CHUNK, CHUNK), 1)
    lane_lt64 = col < 64
    lane_lt64_row = lax.broadcasted_iota(jnp.int32, (1, CHUNK), 1) < 64
    a_neg = -jnp.exp(alog_ref[0])
    dtb = dtb_ref[0]

    def one_pass(d, c):
        r = pl.multiple_of(c * CHUNK, CHUNK)
        rows = pl.ds(r, CHUNK)
        dt = jax.nn.softplus(dt_ref[0, rows, :] + dtb)
        a = dt * a_neg
        if d == 0:
            mask = row >= col
            cs = jnp.dot(mask.astype(F32), a, precision=HIGHEST, preferred_element_type=F32)
            tot = cs[CHUNK - 1:CHUNK, :]
        else:
            mask = row <= col
            cs = jnp.dot(mask.astype(F32), a, precision=HIGHEST, preferred_element_type=F32)
            tot = cs[0:1, :]
        ein = jnp.exp(cs)
        dd = jnp.exp(tot - cs) * dt
        cdec = jnp.exp(tot)
        cs_t = cs.T
        cmat = cm_ref[rows, :]
        btc = bt_ref[c]
        cbm = jnp.dot(cmat, btc, preferred_element_type=F32)
        for p in range(2):
            k0 = 64 * d + DT_REP * 2 * p
            k1 = k0 + DT_REP
            x2 = xs_ref[rows, 128 * p:128 * p + 128]
            dt2 = _lane_pair(dt[:, k0:k0 + 1], dt[:, k1:k1 + 1], lane_lt64)
            dd2 = _lane_pair(dd[:, k0:k0 + 1], dd[:, k1:k1 + 1], lane_lt64)
            ein2 = _lane_pair(ein[:, k0:k0 + 1], ein[:, k1:k1 + 1], lane_lt64)
            cd2 = _lane_pair(cdec[:, k0:k0 + 1], cdec[:, k1:k1 + 1], lane_lt64_row)
            xg2 = (x2 * dt2).astype(BF16)
            xdec2 = (x2 * dd2).astype(BF16)
            ys = []
            for k in (k0, k1):
                diff = cs[:, k:k + 1] - cs_t[k:k + 1, :]
                seg = jnp.where(mask, jnp.exp(diff), 0.0)
                g = (cbm * seg).astype(BF16)
                ys.append(jnp.dot(g, xg2, preferred_element_type=F32))
            ydiag = jnp.where(lane_lt64, ys[0], ys[1])
            st = st_ref[d, p]
            yoff = jnp.dot(cmat, st.astype(BF16), preferred_element_type=F32) * ein2
            yacc_ref[rows, 128 * p:128 * p + 128] += ydiag + yoff
            snew = jnp.dot(btc, xdec2, preferred_element_type=F32)
            st_ref[d, p] = st * cd2 + snew

    def main_body(i, carry):
        one_pass(0, i)
        one_pass(1, nch - 1 - i)
        return carry

    lax.fori_loop(0, nch, main_body, 0)

    def out_body(c, carry):
        rows = pl.ds(pl.multiple_of(c * CHUNK, CHUNK), CHUNK)
        y_ref[0, rows, :] = yacc_ref[rows, :].astype(BF16)
        return carry

    lax.fori_loop(0, nch, out_body, 0)


def ssd_mixer(proj3, dtx3, conv_w, conv_b, dtb_g, alog_g, dsk_g):
    B, L, _ = proj3.shape
    xb = PC_XBC // 256
    bb = (PC_XBC + D_INNER) // 128
    cbk = (PC_XBC + D_INNER + GN) // 128
    return pl.pallas_call(
        _ssd_kernel,
        grid=(B, SSD_GROUPS),
        in_specs=[
            pl.BlockSpec((1, L, 256), lambda b, g: (b, 0, xb + g)),
            pl.BlockSpec((1, L, 128), lambda b, g: (b, 0, bb + g)),
            pl.BlockSpec((1, L, 128), lambda b, g: (b, 0, cbk + g)),
            pl.BlockSpec((1, L, 128), lambda b, g: (b, 0, g)),
            pl.BlockSpec((D_CONV, 256), lambda b, g: (0, g)),
            pl.BlockSpec((D_CONV, 128), lambda b, g: (0, D_INNER // 128 + g)),
            pl.BlockSpec((D_CONV, 128), lambda b, g: (0, (D_INNER + GN) // 128 + g)),
            pl.BlockSpec((1, 256), lambda b, g: (0, g)),
            pl.BlockSpec((1, 128), lambda b, g: (0, D_INNER // 128 + g)),
            pl.BlockSpec((1, 128), lambda b, g: (0, (D_INNER + GN) // 128 + g)),
            pl.BlockSpec((1, 1, 128), lambda b, g: (g, 0, 0)),
            pl.BlockSpec((1, 1, 128), lambda b, g: (g, 0, 0)),
            pl.BlockSpec((1, 1, 256), lambda b, g: (g, 0, 0)),
        ],
        out_specs=pl.BlockSpec((1, L, 256), lambda b, g: (b, 0, g)),
        out_shape=jax.ShapeDtypeStruct((B, L, D_INNER), BF16),
        scratch_shapes=[
            pltpu.VMEM((L + 16, 512), F32),
            pltpu.VMEM((L, 256), F32),
            pltpu.VMEM((L // CHUNK, SSD_STATE, CHUNK), BF16),
            pltpu.VMEM((L, SSD_STATE), BF16),
            pltpu.VMEM((L, 256), F32),
            pltpu.VMEM((2, 2, SSD_STATE, 128), F32),
        ],
        compiler_params=_cparams(2),
        name="ssd_mixer",
    )(proj3, proj3, proj3, dtx3, conv_w, conv_w, conv_w, conv_b, conv_b, conv_b, dtb_g, alog_g, dsk_g)


def _attn_kernel(q_ref, k_ref, v_ref, bias_ref, o_ref, lse_ref, kp_ref, vp_ref, *stage, dil):
    L = q_ref.shape[1]
    M = L // dil
    nt = M // ATT_TQ
    width = q_ref.shape[2]
    nslab = width // 128
    zpad = jnp.zeros((ATT_SPAN, width), BF16)
    kp_ref[0:ATT_SPAN, :] = zpad
    kp_ref[M + ATT_SPAN:M + 2 * ATT_SPAN, :] = zpad
    vp_ref[0:ATT_SPAN, :] = zpad
    vp_ref[M + ATT_SPAN:M + 2 * ATT_SPAN, :] = zpad

    if dil > 1:
        qf_ref, kf_ref, vf_ref, of_ref, lf_ref = stage

        def stage_body(c, carry):
            rows = pl.ds(pl.multiple_of(c * ATT_TQ, ATT_TQ), ATT_TQ)
            for s in range(nslab):
                cols = slice(128 * s, 128 * s + 128)
                qf_ref[s, rows, :] = q_ref[0, rows, cols].astype(F32)
                kf_ref[s, rows, :] = k_ref[0, rows, cols].astype(F32)
                vf_ref[s, rows, :] = v_ref[0, rows, cols].astype(F32)
            return carry

        lax.fori_loop(0, L // ATT_TQ, stage_body, 0)
    else:
        qf_ref = kf_ref = vf_ref = of_ref = lf_ref = None

    def sub_rows(r, t):
        return pl.ds(r + t * (ATT_TQ * dil), ATT_TQ, stride=dil)

    def load_tile(ref, ref_f32, r, t):
        if dil == 1:
            return ref[0, pl.ds(pl.multiple_of(t * ATT_TQ, ATT_TQ), ATT_TQ), :]
        return jnp.concatenate([ref_f32[s, sub_rows(r, t), :] for s in range(nslab)], axis=1).astype(BF16)

    lane = lax.broadcasted_iota(jnp.int32, (ATT_TQ, 128), 1)
    lane_lt64 = lane < 64
    jj = lax.broadcasted_iota(jnp.int32, (ATT_TQ, ATT_TK), 1)

    def residue_body(r, carry):
        def copy_body(t, c2):
            dst = pl.ds(pl.multiple_of(t * ATT_TQ, ATT_TQ) + ATT_SPAN, ATT_TQ)
            kp_ref[dst, :] = load_tile(k_ref, kf_ref, r, t)
            vp_ref[dst, :] = load_tile(v_ref, vf_ref, r, t)
            return c2

        lax.fori_loop(0, nt, copy_body, 0)

        def tile_body(t, c2):
            r0 = pl.multiple_of(t * ATT_TQ, ATT_TQ)
            q = load_tile(q_ref, qf_ref, r, t) * jnp.asarray(ATTN_HEAD_DIM ** -0.5, BF16)
            kw = kp_ref[pl.ds(r0, ATT_TK), :]
            vw = vp_ref[pl.ds(r0, ATT_TK), :]
            in_seq = (jj >= ATT_SPAN - r0) & (jj < M + ATT_SPAN - r0)
            lse_tile = jnp.zeros((ATT_TQ, 128), F32)
            outs = []
            for p in range(HEADS_PER_GROUP // 2):
                cols = slice(128 * p, 128 * p + 128)
                q2, k2, v2 = q[:, cols], kw[:, cols], vw[:, cols]
                pv, dens = [], []
                for hh in range(2):
                    h = 2 * p + hh
                    qm = jnp.where(lane_lt64 if hh == 0 else jnp.logical_not(lane_lt64), q2, jnp.zeros_like(q2))
                    s = lax.dot_general(qm, k2, (((1,), (1,)), ((), ())), preferred_element_type=F32)
                    s = jnp.where(in_seq, s + bias_ref[h], NEG)
                    m = jnp.max(s, axis=-1, keepdims=True)
                    pexp = jnp.exp(s - m)
                    den = jnp.sum(pexp, axis=-1, keepdims=True)
                    pv.append(jnp.dot(pexp.astype(BF16), v2, preferred_element_type=F32))
                    dens.append(den)
                    lse_tile = jnp.where(lane == h, m + jnp.log(den), lse_tile)
                outs.append(jnp.where(lane_lt64, pv[0], pv[1]) / jnp.where(lane_lt64, dens[0], dens[1]))
            if dil == 1:
                o_ref[0, pl.ds(r0, ATT_TQ), :] = jnp.concatenate(outs, axis=1).astype(BF16)
                lse_ref[0, pl.ds(r0, ATT_TQ), :] = lse_tile
            else:
                for p, o2 in enumerate(outs):
                    of_ref[p, sub_rows(r, t), :] = o2
                lf_ref[sub_rows(r, t), :] = lse_tile
            return c2

        lax.fori_loop(0, nt, tile_body, 0)
        return carry

    if dil == 1:
        residue_body(0, 0)
    else:
        lax.fori_loop(0, dil, residue_body, 0)

        def out_body(c, carry):
            rows = pl.ds(pl.multiple_of(c * ATT_TQ, ATT_TQ), ATT_TQ)
            o_ref[0, rows, :] = jnp.concatenate([of_ref[s, rows, :] for s in range(nslab)], axis=1).astype(BF16)
            lse_ref[0, rows, :] = lf_ref[rows, :]
            return carry

        lax.fori_loop(0, L // ATT_TQ, out_body, 0)


def dilated_attention(proj3, bias_g, gi, dil):
    B, L, _ = proj3.shape
    M = L // dil
    width = HEADS_PER_GROUP * ATTN_HEAD_DIM
    nslab = width // 128
    qb, kb, vb = PC_Q // width + gi, PC_K // width + gi, PC_V // width + gi
    scratch = [
        pltpu.VMEM((M + 2 * ATT_SPAN, width), BF16),
        pltpu.VMEM((M + 2 * ATT_SPAN, width), BF16),
    ]
    if dil > 1:
        scratch += [pltpu.VMEM((nslab, L, 128), F32)] * 4 + [pltpu.VMEM((L, 128), F32)]
    return pl.pallas_call(
        functools.partial(_attn_kernel, dil=dil),
        grid=(B,),
        in_specs=[
            pl.BlockSpec((1, L, width), lambda b: (b, 0, qb)),
            pl.BlockSpec((1, L, width), lambda b: (b, 0, kb)),
            pl.BlockSpec((1, L, width), lambda b: (b, 0, vb)),
            pl.BlockSpec((HEADS_PER_GROUP, ATT_TQ, ATT_TK), lambda b: (0, 0, 0)),
        ],
        out_specs=[
            pl.BlockSpec((1, L, width), lambda b: (b, 0, 0)),
            pl.BlockSpec((1, L, 128), lambda b: (b, 0, 0)),
        ],
        out_shape=[
            jax.ShapeDtypeStruct((B, L, width), BF16),
            jax.ShapeDtypeStruct((B, L, 128), F32),
        ],
        scratch_shapes=scratch,
        compiler_params=_cparams(1),
        name=f"dilated_attention_d{dil}",
    )(proj3, proj3, proj3, bias_g)


def _mix_kernel(x_ref, y_ref, z_ref, g_ref, o0_ref, o1_ref, o2_ref, l0_ref, l1_ref, l2_ref,
                nw_ref, wssd_ref, watt_ref, wout_ref, out_ref):
    z = z_ref[...].astype(F32)
    yz = y_ref[...].astype(F32) * (z * jax.nn.sigmoid(z))
    ms = jnp.mean(yz * yz, axis=-1, keepdims=True)
    yn = ((yz * lax.rsqrt(ms + RMS_EPS)) * nw_ref[...]).astype(BF16)
    pa = jnp.dot(yn, wssd_ref[...], preferred_element_type=F32)

    ls = [l0_ref[...], l1_ref[...], l2_ref[...]]
    mx = jnp.maximum(jnp.maximum(ls[0], ls[1]), ls[2])
    es = [jnp.exp(l - mx) for l in ls]
    den = es[0] + es[1] + es[2]
    erow = lax.broadcasted_iota(jnp.int32, (128, ATTN_OUT_WIDTH), 0)
    ecol = lax.broadcasted_iota(jnp.int32, (128, ATTN_OUT_WIDTH), 1)
    expand = (erow == ecol // ATTN_HEAD_DIM).astype(F32)
    yatt = None
    for e, o_ref in zip(es, (o0_ref, o1_ref, o2_ref)):
        wexp = jnp.dot(e / den, expand, precision=HIGHEST, preferred_element_type=F32)
        term = wexp * o_ref[...].astype(F32)
        yatt = term if yatt is None else yatt + term
    pb = jnp.dot(yatt.astype(BF16), watt_ref[...], preferred_element_type=F32)

    gates = jax.nn.sigmoid(g_ref[...].astype(F32))
    merged = gates[:, :D_MODEL] * pa + gates[:, D_MODEL:] * pb
    out_ref[...] = x_ref[...] + jnp.dot(merged.astype(BF16), wout_ref[...], preferred_element_type=F32)


def mix_project(x2d, yssd, proj, outs, lses, norm_w, w_ssd, w_att, w_out, tm=512):
    T = x2d.shape[0]
    row = lambda i: (i, 0)
    const = lambda i: (0, 0)
    return pl.pallas_call(
        _mix_kernel,
        grid=(T // tm,),
        in_specs=[
            pl.BlockSpec((tm, D_MODEL), row),
            pl.BlockSpec((tm, D_INNER), row),
            pl.BlockSpec((tm, D_INNER), lambda i: (i, PC_Z // D_INNER)),
            pl.BlockSpec((tm, 2 * D_MODEL), lambda i: (i, PC_GATE // (2 * D_MODEL))),
            pl.BlockSpec((tm, ATTN_OUT_WIDTH), row),
            pl.BlockSpec((tm, ATTN_OUT_WIDTH), row),
            pl.BlockSpec((tm, ATTN_OUT_WIDTH), row),
            pl.BlockSpec((tm, 128), row),
            pl.BlockSpec((tm, 128), row),
            pl.BlockSpec((tm, 128), row),
            pl.BlockSpec((1, D_INNER), const),
            pl.BlockSpec((D_INNER, D_MODEL), const),
            pl.BlockSpec((ATTN_OUT_WIDTH, D_MODEL), const),
            pl.BlockSpec((D_MODEL, D_MODEL), const),
        ],
        out_specs=pl.BlockSpec((tm, D_MODEL), row),
        out_shape=jax.ShapeDtypeStruct((T, D_MODEL), F32),
        compiler_params=_cparams(1),
        name="mix_project",
    )(x2d, yssd, proj, proj, outs[0], outs[1], outs[2], lses[0], lses[1], lses[2], norm_w, w_ssd, w_att, w_out)


ROUTE_LANES = 128
N_CAND = 80


def _extract_top(vals, ids, n, big):
    out_v, out_i = [], []
    for _ in range(n):
        m = jnp.max(vals, axis=0, keepdims=True)
        pick = jnp.min(jnp.where(vals == m, ids, big), axis=0, keepdims=True)
        out_v.append(m)
        out_i.append(pick)
        vals = jnp.where(ids == pick, -jnp.inf, vals)
    return out_v, out_i


def _stack_rows(rows):
    n = len(rows)
    rid = lax.broadcasted_iota(jnp.int32, (n, rows[0].shape[1]), 0)
    out = jnp.broadcast_to(rows[0], rid.shape)
    for k in range(1, n):
        out = jnp.where(rid == k, rows[k], out)
    return out


def _peer_route_kernel(x_ref, nw_ref, wq_ref, keys_ref, xn_ref, idx_ref, g_ref, qt_ref, it_ref, gt_ref):
    tm = x_ref.shape[0]
    x = x_ref[...]
    ms = jnp.mean(x * x, axis=-1, keepdims=True)
    xn = ((x * lax.rsqrt(ms + RMS_EPS)) * nw_ref[...]).astype(BF16)
    xn_ref[...] = xn
    qt_ref[...] = lax.dot_general(wq_ref[...], xn, (((1,), (1,)), ((), ())),
                                  preferred_element_type=F32).astype(BF16)

    key_id = lax.broadcasted_iota(jnp.int32, (PEER_N_KEYS, ROUTE_LANES), 0).astype(F32)
    r80 = lax.broadcasted_iota(jnp.int32, (N_CAND, ROUTE_LANES), 0)
    rank_a = jnp.where(r80 < 16, 0, jnp.where(r80 < 72, 1 + lax.shift_right_logical(r80 - 16, 3), r80 - 64))
    rank_b = jnp.where(r80 < 16, r80, jnp.where(r80 < 72, (r80 - 16) & 7, 0))
    code = (rank_a * PEER_TOPK + rank_b).astype(F32)

    def route(h, lane0):
        lanes = slice(lane0, lane0 + ROUTE_LANES)
        sv, si = [], []
        for s in range(2):
            qhs = qt_ref[pl.ds(pl.multiple_of((2 * h + s) * PEER_HALF, PEER_HALF), PEER_HALF), lanes]
            sc = jnp.dot(keys_ref[s], qhs, preferred_element_type=F32)
            v, i = _extract_top(sc, key_id, PEER_TOPK, float(PEER_N_KEYS))
            sv.append(v)
            si.append(i)
        sv1 = _stack_rows(sv[1])
        si1 = _stack_rows(si[1])
        cand = [sv[0][0] + sv1]
        eid = [si[0][0] * PEER_N_KEYS + si1]
        for a in range(1, 8):
            cand.append(sv[0][a] + sv1[0:8])
            eid.append(si[0][a] * PEER_N_KEYS + si1[0:8])
        cand.append(_stack_rows(sv[0][8:16]) + sv1[0:1])
        eid.append(_stack_rows(si[0][8:16]) * PEER_N_KEYS + si1[0:1])
        cand = jnp.concatenate(cand, axis=0)
        eid = jnp.concatenate(eid, axis=0)
        cv, cc = _extract_top(cand, code, PEER_TOPK, 256.0)
        picked = [jnp.max(jnp.where(code == c_k, eid, -1.0), axis=0, keepdims=True) for c_k in cc]
        cvs = _stack_rows(cv)
        ex = jnp.exp(cvs - cvs[0:1])
        gate = ex / jnp.sum(ex, axis=0, keepdims=True)
        rows = pl.ds(pl.multiple_of(h * PEER_TOPK, PEER_TOPK), PEER_TOPK)
        it_ref[rows, lanes] = _stack_rows(picked)
        gt_ref[rows, lanes] = gate

    def body(h, carry):
        for lane0 in range(0, tm, ROUTE_LANES):
            route(h, lane0)
        return carry

    lax.fori_loop(0, PEER_HEADS, body, 0)
    idx_ref[...] = it_ref[...].T.astype(jnp.int32)
    g_ref[...] = gt_ref[...].T


def peer_route(x2d, norm_w, wq_t, keys, tm=256):
    T = x2d.shape[0]
    npick = PEER_HEADS * PEER_TOPK
    return pl.pallas_call(
        _peer_route_kernel,
        grid=(T // tm,),
        in_specs=[
            pl.BlockSpec((tm, D_MODEL), lambda i: (i, 0)),
            pl.BlockSpec((1, D_MODEL), lambda i: (0, 0)),
            pl.BlockSpec((2 * PEER_HEADS * PEER_HALF, D_MODEL), lambda i: (0, 0)),
            pl.BlockSpec((2, PEER_N_KEYS, PEER_HALF), lambda i: (0, 0, 0)),
        ],
        out_specs=[
            pl.BlockSpec((tm, D_MODEL), lambda i: (i, 0)),
            pl.BlockSpec((tm, npick), lambda i: (i, 0)),
            pl.BlockSpec((tm, npick), lambda i: (i, 0)),
        ],
        out_shape=[
            jax.ShapeDtypeStruct((T, D_MODEL), BF16),
            jax.ShapeDtypeStruct((T, npick), jnp.int32),
            jax.ShapeDtypeStruct((T, npick), F32),
        ],
        scratch_shapes=[
            pltpu.VMEM((2 * PEER_HEADS * PEER_HALF, tm), BF16),
            pltpu.VMEM((npick, tm), F32),
            pltpu.VMEM((npick, tm), F32),
        ],
        compiler_params=_cparams(1),
        name="peer_route",
    )(x2d, norm_w, wq_t, keys)


GATE_PITCH = 136
GATE_UNROLL = 8


def _peer_gates_kernel(idx_ref, g_ref, gd_ref, g3_ref):
    tm = idx_ref.shape[0]
    rid = lax.broadcasted_iota(jnp.int32, (PEER_N_KEYS, 128), 0)

    def body(tb, carry):
        t0 = pl.multiple_of(tb * GATE_UNROLL, GATE_UNROLL)
        e = idx_ref[pl.ds(t0, GATE_UNROLL), :]
        g = g_ref[pl.ds(t0, GATE_UNROLL), :]
        i1 = lax.shift_right_logical(e, 7)
        i2 = e & (PEER_N_KEYS - 1)
        for u in range(GATE_UNROLL):
            a_t = jnp.where(rid == i1[u:u + 1], g[u:u + 1], 0.0).astype(BF16)
            b_t = jnp.where(rid == i2[u:u + 1], 1.0, 0.0).astype(BF16)
            gd = lax.dot_general(a_t, b_t, (((1,), (1,)), ((), ())), preferred_element_type=F32)
            g3_ref[pl.ds(pl.multiple_of((t0 + u) * GATE_PITCH, 8), PEER_N_KEYS), :] = gd
        return carry

    lax.fori_loop(0, tm // GATE_UNROLL, body, 0)
    for a in range(PEER_N_KEYS):
        gd_ref[:, a * PEER_N_KEYS:(a + 1) * PEER_N_KEYS] = g3_ref[pl.ds(a, tm, stride=GATE_PITCH), :].astype(BF16)


def peer_gates(idx, g, tm=128):
    T, npick = idx.shape
    return pl.pallas_call(
        _peer_gates_kernel,
        grid=(T // tm,),
        in_specs=[pl.BlockSpec((tm, npick), lambda i: (i, 0)), pl.BlockSpec((tm, npick), lambda i: (i, 0))],
        out_specs=pl.BlockSpec((tm, PEER_EXPERTS), lambda i: (i, 0)),
        out_shape=jax.ShapeDtypeStruct((T, PEER_EXPERTS), BF16),
        scratch_shapes=[pltpu.VMEM((tm * GATE_PITCH, PEER_N_KEYS), F32)],
        compiler_params=_cparams(1),
        name="peer_gates",
    )(idx, g)


def _peer_experts_kernel(x_ref, xn_ref, gd_ref, ut_ref, v_ref, fw_ref, out_ref, acc_ref, *, final_norm):
    j = pl.program_id(1)

    @pl.when(j == 0)
    def _():
        acc_ref[...] = jnp.zeros_like(acc_ref)

    z = jnp.dot(xn_ref[...], ut_ref[...], preferred_element_type=F32)
    hid = 0.5 * z * (1.0 + lax.erf(z * np.float32(np.sqrt(0.5))))
    w = (gd_ref[...].astype(F32) * hid).astype(BF16)
    acc_ref[...] += jnp.dot(w, v_ref[...], preferred_element_type=F32)

    @pl.when(j == pl.num_programs(1) - 1)
    def _():
        y = x_ref[...] + acc_ref[...]
        if final_norm:
            ms = jnp.mean(y * y, axis=-1, keepdims=True)
            y = (y * lax.rsqrt(ms + RMS_EPS)) * fw_ref[...]
        out_ref[...] = y


def peer_experts(x2d, xn, gd, u_t, v, final_w, final_norm, tm=512, te=1024):
    T = x2d.shape[0]
    return pl.pallas_call(
        functools.partial(_peer_experts_kernel, final_norm=final_norm),
        grid=(T // tm, PEER_EXPERTS // te),
        in_specs=[
            pl.BlockSpec((tm, D_MODEL), lambda i, j: (i, 0)),
            pl.BlockSpec((tm, D_MODEL), lambda i, j: (i, 0)),
            pl.BlockSpec((tm, te), lambda i, j: (i, j)),
            pl.BlockSpec((D_MODEL, te), lambda i, j: (0, j)),
            pl.BlockSpec((te, D_MODEL), lambda i, j: (j, 0)),
            pl.BlockSpec((1, D_MODEL), lambda i, j: (0, 0)),
        ],
        out_specs=pl.BlockSpec((tm, D_MODEL), lambda i, j: (i, 0)),
        out_shape=jax.ShapeDtypeStruct((T, D_MODEL), F32),
        scratch_shapes=[pltpu.VMEM((tm, D_MODEL), F32)],
        compiler_params=_cparams(2),
        name="peer_experts_final" if final_norm else "peer_experts",
    )(x2d, xn, gd, u_t, v, final_w)


def _t5_bucket(rel):
    half = REL_BUCKETS // 2
    max_exact = half // 2
    n = np.abs(rel)
    large = max_exact + (np.log(np.maximum(n, 1) / max_exact) / np.log(REL_MAX_DISTANCE / max_exact)
                         * (half - max_exact)).astype(np.int32)
    large = np.minimum(large, half - 1)
    return np.where(rel > 0, half, 0) + np.where(n < max_exact, n, large)


def prep_attn_bias(rel_bias, gi, dil):
    i = np.arange(ATT_TQ)[:, None]
    j = np.arange(ATT_TK)[None, :]
    rel = j - ATT_SPAN - i
    bucket = _t5_bucket(rel * dil)
    table = rel_bias[bucket][:, :, gi * HEADS_PER_GROUP:(gi + 1) * HEADS_PER_GROUP]
    table = jnp.where((np.abs(rel) <= ATT_SPAN)[:, :, None], table.astype(F32), NEG)
    return jnp.transpose(table, (2, 0, 1))


def _dt_col_index():
    g = np.arange(SSD_GROUPS)[:, None, None, None]
    d = np.arange(2)[None, :, None, None]
    h = np.arange(4)[None, None, :, None]
    rep = np.zeros((1, 1, 1, DT_REP), np.int64)
    return (d * SSD_HEADS + 4 * g + h + rep).reshape(-1)


def prep_w_in(w_in_l):
    w_main = jnp.concatenate(
        [w_in_l[:, :OFF_XBC], w_in_l[:, OFF_GATE:], w_in_l[:, OFF_XBC:OFF_DT], w_in_l[:, OFF_Q:OFF_GATE]], axis=1
    ).astype(BF16)
    w_dtx = w_in_l[:, OFF_DT + _dt_col_index()].astype(BF16)
    return w_main, w_dtx


def prep_ssd_params(dt_bias_l, a_log_l, d_skip_l):
    idx = _dt_col_index()
    dtb_g = dt_bias_l.reshape(-1)[idx].reshape(SSD_GROUPS, 1, 128)
    alog_g = a_log_l.reshape(-1)[idx].reshape(SSD_GROUPS, 1, 128)
    dsk_g = jnp.repeat(d_skip_l, SSD_HEAD_DIM).reshape(SSD_GROUPS, 1, 256)
    return dtb_g, alog_g, dsk_g


def trunk_layer(x2d, B, L, biases, norm1_w, w_in, conv_w, conv_b, a_log, dt_bias, d_skip, ssd_norm_w,
                w_ssd_out, w_attn_out, w_out, norm2_w, w_query, sub_keys, expert_u, expert_v, final_w, final_norm):
    w_main, w_dtx = prep_w_in(w_in)
    proj, dtx = in_proj(x2d, norm1_w.reshape(1, -1), w_main, w_dtx)
    proj3 = proj.reshape(B, L, PROJ_COLS)
    yssd = ssd_mixer(proj3, dtx.reshape(B, L, DTX_COLS), conv_w, conv_b.reshape(1, -1),
                     *prep_ssd_params(dt_bias, a_log, d_skip))
    outs, lses = [], []
    for gi, (_, dil) in enumerate(DILATION_GROUPS):
        o, lse = dilated_attention(proj3, biases[gi], gi, dil)
        outs.append(o.reshape(B * L, ATTN_OUT_WIDTH))
        lses.append(lse.reshape(B * L, 128))
    x1 = mix_project(x2d, yssd.reshape(B * L, D_INNER), proj, outs, lses, ssd_norm_w.reshape(1, -1),
                     w_ssd_out.astype(BF16), w_attn_out.astype(BF16), w_out.astype(BF16))
    xn, idx, gate = peer_route(x1, norm2_w.reshape(1, -1), w_query.T.astype(BF16), sub_keys.astype(BF16))
    gd = peer_gates(idx, gate)
    return peer_experts(x1, xn, gd, expert_u.T.astype(BF16), expert_v.astype(BF16), final_w.reshape(1, -1), final_norm)


def kernel(x_prompt, x_sample, rel_bias, norm1_w, w_in, conv_w, conv_b, A_log, dt_bias, D_skip, ssd_norm_w,
           w_ssd_out, w_attn_out, w_out, norm2_w, w_query, sub_keys, expert_u, expert_v, final_norm_w):
    nb_prompt = x_prompt.shape[0]
    x = jnp.concatenate([x_prompt, x_sample], axis=0)
    B, L, _ = x.shape
    x2d = x.reshape(B * L, D_MODEL)
    biases = [prep_attn_bias(rel_bias, gi, dil) for gi, (_, dil) in enumerate(DILATION_GROUPS)]
    for l in range(DEPTH):
        x2d = trunk_layer(x2d, B, L, biases, norm1_w[l], w_in[l], conv_w[l], conv_b[l], A_log[l], dt_bias[l],
                          D_skip[l], ssd_norm_w[l], w_ssd_out[l], w_attn_out[l], w_out[l], norm2_w[l], w_query[l],
                          sub_keys[l], expert_u[l], expert_v[l], final_norm_w, l == DEPTH - 1)
    y = x2d.reshape(B, L, D_MODEL)
    return y[:nb_prompt], y[nb_prompt:]
```

```python
import functools

import numpy as np
import jax
import jax.numpy as jnp
from jax import lax
from jax.experimental import pallas as pl
from jax.experimental.pallas import tpu as pltpu

F32 = jnp.float32
BF16 = jnp.bfloat16
HIGHEST = lax.Precision.HIGHEST

D_MODEL = 1024
DEPTH = 2
D_INNER = 2048
SSD_HEADS = 32
SSD_HEAD_DIM = 64
SSD_GROUPS = 8
SSD_STATE = 128
CHUNK = 128
D_CONV = 5
GN = SSD_GROUPS * SSD_STATE
CONV_DIM = D_INNER + 2 * GN
ATTN_HEAD_DIM = 64
HEADS_PER_GROUP = 8
DILATION_GROUPS = ((128, 1), (512, 4), (2048, 16))
ATTN_WIDTH = 1536
ATTN_OUT_WIDTH = 512
REL_BUCKETS = 32
REL_MAX_DISTANCE = 1024
PEER_HEADS = 8
PEER_N_KEYS = 128
PEER_EXPERTS = PEER_N_KEYS * PEER_N_KEYS
PEER_HALF = 128
PEER_TOPK = 16
RMS_EPS = 1e-6

OFF_XBC = D_INNER
OFF_DT = OFF_XBC + CONV_DIM
OFF_Q = OFF_DT + 2 * SSD_HEADS

OFF_K = OFF_Q + ATTN_WIDTH
OFF_V = OFF_K + ATTN_WIDTH
OFF_GATE = OFF_V + ATTN_WIDTH

PC_Z = 0
PC_GATE = D_INNER
PC_XBC = PC_GATE + 2 * D_MODEL
PC_Q = PC_XBC + CONV_DIM
PC_K = PC_Q + ATTN_WIDTH
PC_V = PC_K + ATTN_WIDTH
PROJ_COLS = PC_V + ATTN_WIDTH
DT_REP = 16
DTX_COLS = SSD_GROUPS * 128

ATT_SPAN = 64
ATT_TQ = 128
ATT_TK = 256
NEG = -1e30

VMEM_LIMIT = 56 * 1024 * 1024


def _cparams(n_axes):
    return pltpu.CompilerParams(
        dimension_semantics=("arbitrary",) * n_axes, vmem_limit_bytes=VMEM_LIMIT
    )


def _inproj_kernel(x_ref, nw_ref, w_ref, wdt_ref, o_ref, dt_ref, xn_ref):
    @pl.when(pl.program_id(1) == 0)
    def _():
        x = x_ref[...]
        ms = jnp.mean(x * x, axis=-1, keepdims=True)
        xn = ((x * lax.rsqrt(ms + RMS_EPS)) * nw_ref[...]).astype(BF16)
        xn_ref[...] = xn
        dt_ref[...] = jnp.dot(xn, wdt_ref[...], preferred_element_type=F32)

    o_ref[...] = jnp.dot(xn_ref[...], w_ref[...], preferred_element_type=F32).astype(BF16)


def in_proj(x2d, norm_w, w_main, w_dtx, tm=1024, tn=1280):
    T = x2d.shape[0]
    nc = w_main.shape[1]
    return pl.pallas_call(
        _inproj_kernel,
        grid=(T // tm, nc // tn),
        in_specs=[
            pl.BlockSpec((tm, D_MODEL), lambda i, j: (i, 0)),
            pl.BlockSpec((1, D_MODEL), lambda i, j: (0, 0)),
            pl.BlockSpec((D_MODEL, tn), lambda i, j: (0, j)),
            pl.BlockSpec((D_MODEL, DTX_COLS), lambda i, j: (0, 0)),
        ],
        out_specs=[
            pl.BlockSpec((tm, tn), lambda i, j: (i, j)),
            pl.BlockSpec((tm, DTX_COLS), lambda i, j: (i, 0)),
        ],
        out_shape=[
            jax.ShapeDtypeStruct((T, nc), BF16),
            jax.ShapeDtypeStruct((T, DTX_COLS), F32),
        ],
        scratch_shapes=[pltpu.VMEM((tm, D_MODEL), BF16)],
        compiler_params=_cparams(2),
        name="in_proj",
    )(x2d, norm_w, w_main, w_dtx)


SSD_HALO = 64


def _split3(x):
    hi = x.astype(BF16).astype(F32)
    r1 = x - hi
    mid = r1.astype(BF16).astype(F32)
    lo = (r1 - mid).astype(BF16).astype(F32)
    return hi, mid, lo


def _ssd_kernel(x_ref, b_ref, c_ref, dt_ref, cwx_ref, cwb_ref, cwc_ref, cbx_ref, cbb_ref, cbc_ref,
                dtb_ref, alog_ref, dsk_ref, y_ref,
                pad_ref, xs_ref, bt_ref, cm_ref, yacc_ref, st_ref, pk_ref, ng_ref):
    L = x_ref.shape[1]
    nch = L // CHUNK
    cw = jnp.concatenate([cwx_ref[...], cwb_ref[...], cwc_ref[...]], axis=1)
    cb = jnp.concatenate([cbx_ref[...], cbb_ref[...], cbc_ref[...]], axis=1)
    dsk = dsk_ref[0]

    zhalo = jnp.zeros((SSD_HALO, 512), BF16)
    pad_ref[0:SSD_HALO, :] = zhalo
    pad_ref[L + SSD_HALO:L + 2 * SSD_HALO, :] = zhalo

    def copy_body(c, carry):
        r = pl.multiple_of(c * CHUNK, CHUNK)
        dst = pl.ds(r + SSD_HALO, CHUNK)
        pad_ref[dst, 0:256] = x_ref[0, pl.ds(r, CHUNK), :]
        pad_ref[dst, 256:384] = b_ref[0, pl.ds(r, CHUNK), :]
        pad_ref[dst, 384:512] = c_ref[0, pl.ds(r, CHUNK), :]
        return carry

    lax.fori_loop(0, nch, copy_body, 0)

    taps = (-2, -1, 1, 2)
    srow = lax.broadcasted_iota(jnp.int32, (len(taps) * CHUNK, CHUNK + 2 * SSD_HALO), 0)
    scol = lax.broadcasted_iota(jnp.int32, (len(taps) * CHUNK, CHUNK + 2 * SSD_HALO), 1)
    soff = jnp.where(srow < CHUNK, taps[0], jnp.where(srow < 2 * CHUNK, taps[1], jnp.where(srow < 3 * CHUNK, taps[2], taps[3])))
    shift_mat = (scol == (srow & (CHUNK - 1)) + SSD_HALO + soff).astype(BF16)

    def conv_body(c, carry):
        r = pl.multiple_of(c * CHUNK, CHUNK)
        win = pad_ref[pl.ds(r, CHUNK + 2 * SSD_HALO), :]
        sh = jnp.dot(shift_mat, win, preferred_element_type=F32)
        acc = win[SSD_HALO:SSD_HALO + CHUNK, :].astype(F32) * cw[2:3, :] + cb
        for i, w in enumerate((0, 1, 3, 4)):
            acc = acc + sh[i * CHUNK:(i + 1) * CHUNK, :] * cw[w:w + 1, :]
        act = acc * jax.nn.sigmoid(acc)
        xs = act[:, 0:256]
        xs_ref[pl.ds(r, CHUNK), :] = xs
        yacc_ref[pl.ds(r, CHUNK), :] = xs * dsk
        bt_ref[c] = act[:, 256:384].T.astype(BF16)
        cm_ref[pl.ds(r, CHUNK), :] = act[:, 384:512].astype(BF16)
        return carry

    lax.fori_loop(0, nch, conv_body, 0, unroll=2)

    st_ref[...] = jnp.zeros_like(st_ref)

    row = lax.broadcasted_iota(jnp.int32, (CHUNK, CHUNK), 0)
    col = lax.broadcasted_iota(jnp.int32, (CHUNK, CHUNK), 1)
    lane_lt64 = col < 64
    m16 = col & (DT_REP - 1)
    a_neg = -jnp.exp(alog_ref[0])
    dtb = dtb_ref[0]
    erow = lax.broadcasted_iota(jnp.int32, (CHUNK, 2 * CHUNK), 0)
    ecol = lax.broadcasted_iota(jnp.int32, (CHUNK, 2 * CHUNK), 1)

    def selectors(d):
        rel = erow - (64 * d + DT_REP * lax.shift_right_logical(ecol, 6))
        sel_cs = ((rel >= 0) & (rel < 3)).astype(BF16)
        sel_dt = ((rel >= 6) & (rel < 9)).astype(BF16)
        return sel_cs, sel_dt

    sels = (selectors(0), selectors(1))
    tri3 = tuple(jnp.concatenate([m.astype(BF16)] * 3, axis=1) for m in (row >= col, row <= col))

    head_rows = [((row >= DT_REP * k) & (row < DT_REP * k + 6)).astype(BF16) for k in range(8)]

    def prep_body(c, carry):
        rows = pl.ds(pl.multiple_of(c * CHUNK, CHUNK), CHUNK)
        dt = jax.nn.softplus(dt_ref[0, rows, :] + dtb)
        a = dt * a_neg
        a3 = jnp.concatenate([t.astype(BF16) for t in _split3(a)], axis=0)
        cs = jnp.where(lane_lt64,
                       jnp.dot(tri3[0], a3, preferred_element_type=F32),
                       jnp.dot(tri3[1], a3, preferred_element_type=F32))
        c_hi, c_mid, c_lo = _split3(cs)
        d_hi, d_mid, d_lo = _split3(dt)
        packed = jnp.where(m16 == 0, c_hi, jnp.where(m16 == 1, c_mid, jnp.where(m16 == 2, c_lo, jnp.where(
            m16 < 6, 1.0, jnp.where(m16 == 6, d_hi, jnp.where(m16 == 7, d_mid, jnp.where(m16 == 8, d_lo, 0.0)))))))
        pk_ref[c] = packed.astype(BF16)
        negs = jnp.where(m16 < 3, 1.0, jnp.where(m16 == 3, -c_hi, jnp.where(m16 == 4, -c_mid, jnp.where(
            m16 == 5, -c_lo, 0.0))))
        ng_ref[c] = negs.T.astype(BF16)
        return carry

    lax.fori_loop(0, nch, prep_body, 0, unroll=2)

    def one_pass(d, c):
        r = pl.multiple_of(c * CHUNK, CHUNK)
        rows = pl.ds(r, CHUNK)
        mask = (row >= col) if d == 0 else (row <= col)
        last = CHUNK - 1 if d == 0 else 0
        packed = pk_ref[c]
        negs_t = ng_ref[c]
        sel_cs, sel_dt = sels[d]
        cs2 = jnp.dot(packed, sel_cs, preferred_element_type=F32)
        dt2 = jnp.dot(packed, sel_dt, preferred_element_type=F32)
        tot2 = cs2[last:last + 1, :]
        ein2 = jnp.exp(cs2)
        dd2 = jnp.exp(tot2 - cs2) * dt2
        cd2 = jnp.exp(tot2)
        cmat = cm_ref[rows, :]
        btc = bt_ref[c]
        cbm = jnp.dot(cmat, btc, preferred_element_type=F32)
        for p in range(2):
            cols = slice(128 * p, 128 * p + 128)
            x2 = xs_ref[rows, cols]
            xg2 = (x2 * dt2[:, cols]).astype(BF16)
            xdec2 = (x2 * dd2[:, cols]).astype(BF16)
            ys = []
            for hh in range(2):
                rhs = negs_t * head_rows[4 * d + 2 * p + hh]
                diff = jnp.dot(packed, rhs, preferred_element_type=F32)
                seg = jnp.where(mask, jnp.exp(diff), 0.0)
                g = (cbm * seg).astype(BF16)
                ys.append(jnp.dot(g, xg2, preferred_element_type=F32))
            ydiag = jnp.where(lane_lt64, ys[0], ys[1])
            st = st_ref[d, p]
            yoff = jnp.dot(cmat, st.astype(BF16), preferred_element_type=F32) * ein2[:, cols]
            yacc_ref[rows, cols] += ydiag + yoff
            snew = jnp.dot(btc, xdec2, preferred_element_type=F32)
            st_ref[d, p] = st * cd2[:, cols] + snew

    def main_body(i, carry):
        one_pass(0, 2 * i)
        one_pass(1, nch - 1 - 2 * i)
        one_pass(0, 2 * i + 1)
        one_pass(1, nch - 2 - 2 * i)
        return carry

    lax.fori_loop(0, nch // 2, main_body, 0)

    def out_body(c, carry):
        rows = pl.ds(pl.multiple_of(c * CHUNK, CHUNK), CHUNK)
        y_ref[0, rows, :] = yacc_ref[rows, :].astype(BF16)
        return carry

    lax.fori_loop(0, nch, out_body, 0)


def ssd_mixer(proj3, dtx3, conv_w, conv_b, dtb_g, alog_g, dsk_g):
    B, L, _ = proj3.shape
    xb = PC_XBC // 256
    bb = (PC_XBC + D_INNER) // 128
    cbk = (PC_XBC + D_INNER + GN) // 128
    return pl.pallas_call(
        _ssd_kernel,
        grid=(B, SSD_GROUPS),
        in_specs=[
            pl.BlockSpec((1, L, 256), lambda b, g: (b, 0, xb + g)),
            pl.BlockSpec((1, L, 128), lambda b, g: (b, 0, bb + g)),
            pl.BlockSpec((1, L, 128), lambda b, g: (b, 0, cbk + g)),
            pl.BlockSpec((1, L, 128), lambda b, g: (b, 0, g)),
            pl.BlockSpec((D_CONV, 256), lambda b, g: (0, g)),
            pl.BlockSpec((D_CONV, 128), lambda b, g: (0, D_INNER // 128 + g)),
            pl.BlockSpec((D_CONV, 128), lambda b, g: (0, (D_INNER + GN) // 128 + g)),
            pl.BlockSpec((1, 256), lambda b, g: (0, g)),
            pl.BlockSpec((1, 128), lambda b, g: (0, D_INNER // 128 + g)),
            pl.BlockSpec((1, 128), lambda b, g: (0, (D_INNER + GN) // 128 + g)),
            pl.BlockSpec((1, 1, 128), lambda b, g: (g, 0, 0)),
            pl.BlockSpec((1, 1, 128), lambda b, g: (g, 0, 0)),
            pl.BlockSpec((1, 1, 256), lambda b, g: (g, 0, 0)),
        ],
        out_specs=pl.BlockSpec((1, L, 256), lambda b, g: (b, 0, g)),
        out_shape=jax.ShapeDtypeStruct((B, L, D_INNER), BF16),
        scratch_shapes=[
            pltpu.VMEM((L + 2 * SSD_HALO, 512), BF16),
            pltpu.VMEM((L, 256), F32),
            pltpu.VMEM((L // CHUNK, SSD_STATE, CHUNK), BF16),
            pltpu.VMEM((L, SSD_STATE), BF16),
            pltpu.VMEM((L, 256), F32),
            pltpu.VMEM((2, 2, SSD_STATE, 128), F32),
            pltpu.VMEM((L // CHUNK, CHUNK, 128), BF16),
            pltpu.VMEM((L // CHUNK, 128, CHUNK), BF16),
        ],
        compiler_params=_cparams(2),
        name="ssd_mixer",
    )(proj3, proj3, proj3, dtx3, conv_w, conv_w, conv_w, conv_b, conv_b, conv_b, dtb_g, alog_g, dsk_g)


def _attn_kernel(q_ref, k_ref, v_ref, bias_ref, o_ref, lse_ref, kp_ref, vp_ref, *stage, dil):
    L = q_ref.shape[1]
    M = L // dil
    nt = M // ATT_TQ
    width = q_ref.shape[2]
    nslab = width // 128
    zpad = jnp.zeros((ATT_SPAN, width), BF16)
    kp_ref[0:ATT_SPAN, :] = zpad
    kp_ref[M + ATT_SPAN:M + 2 * ATT_SPAN, :] = zpad
    vp_ref[0:ATT_SPAN, :] = zpad
    vp_ref[M + ATT_SPAN:M + 2 * ATT_SPAN, :] = zpad

    if dil > 1:
        qf_ref, kf_ref, vf_ref, of_ref, lf_ref = stage

        def stage_body(c, carry):
            rows = pl.ds(pl.multiple_of(c * ATT_TQ, ATT_TQ), ATT_TQ)
            for s in range(nslab):
                cols = slice(128 * s, 128 * s + 128)
                qf_ref[s, rows, :] = q_ref[0, rows, cols].astype(F32)
                kf_ref[s, rows, :] = k_ref[0, rows, cols].astype(F32)
                vf_ref[s, rows, :] = v_ref[0, rows, cols].astype(F32)
            return carry

        lax.fori_loop(0, L // ATT_TQ, stage_body, 0)
    else:
        qf_ref = kf_ref = vf_ref = of_ref = lf_ref = None

    def sub_rows(r, t):
        return pl.ds(r + t * (ATT_TQ * dil), ATT_TQ, stride=dil)

    def load_tile(ref, ref_f32, r, t):
        if dil == 1:
            return ref[0, pl.ds(pl.multiple_of(t * ATT_TQ, ATT_TQ), ATT_TQ), :]
        return jnp.concatenate([ref_f32[s, sub_rows(r, t), :] for s in range(nslab)], axis=1).astype(BF16)

    lane = lax.broadcasted_iota(jnp.int32, (ATT_TQ, 128), 1)
    lane_lt64 = lane < 64
    jj = lax.broadcasted_iota(jnp.int32, (ATT_TQ, ATT_TK), 1)

    def residue_body(r, carry):
        def copy_body(t, c2):
            dst = pl.ds(pl.multiple_of(t * ATT_TQ, ATT_TQ) + ATT_SPAN, ATT_TQ)
            kp_ref[dst, :] = load_tile(k_ref, kf_ref, r, t)
            vp_ref[dst, :] = load_tile(v_ref, vf_ref, r, t)
            return c2

        lax.fori_loop(0, nt, copy_body, 0)

        def tile_body(t, c2):
            r0 = pl.multiple_of(t * ATT_TQ, ATT_TQ)
            q = load_tile(q_ref, qf_ref, r, t) * jnp.asarray(ATTN_HEAD_DIM ** -0.5, BF16)
            kw = kp_ref[pl.ds(r0, ATT_TK), :]
            vw = vp_ref[pl.ds(r0, ATT_TK), :]
            in_seq = (jj >= ATT_SPAN - r0) & (jj < M + ATT_SPAN - r0)
            lse_tile = jnp.zeros((ATT_TQ, 128), F32)
            outs = []
            for p in range(HEADS_PER_GROUP // 2):
                cols = slice(128 * p, 128 * p + 128)
                q2, k2, v2 = q[:, cols], kw[:, cols], vw[:, cols]
                pv, dens = [], []
                for hh in range(2):
                    h = 2 * p + hh
                    qm = jnp.where(lane_lt64 if hh == 0 else jnp.logical_not(lane_lt64), q2, jnp.zeros_like(q2))
                    s = lax.dot_general(qm, k2, (((1,), (1,)), ((), ())), preferred_element_type=F32)
                    s = jnp.where(in_seq, s + bias_ref[h], NEG)
                    m = jnp.max(s, axis=-1, keepdims=True)
                    pexp = jnp.exp(s - m)
                    den = jnp.sum(pexp, axis=-1, keepdims=True)
                    pv.append(jnp.dot(pexp.astype(BF16), v2, preferred_element_type=F32))
                    dens.append(den)
                    lse_tile = jnp.where(lane == h, m + jnp.log(den), lse_tile)
                outs.append(jnp.where(lane_lt64, pv[0], pv[1]) / jnp.where(lane_lt64, dens[0], dens[1]))
            if dil == 1:
                o_ref[0, pl.ds(r0, ATT_TQ), :] = jnp.concatenate(outs, axis=1).astype(BF16)
                lse_ref[0, pl.ds(r0, ATT_TQ), :] = lse_tile
            else:
                for p, o2 in enumerate(outs):
                    of_ref[p, sub_rows(r, t), :] = o2
                lf_ref[sub_rows(r, t), :] = lse_tile
            return c2

        lax.fori_loop(0, nt, tile_body, 0)
        return carry

    if dil == 1:
        residue_body(0, 0)
    else:
        lax.fori_loop(0, dil, residue_body, 0)

        def out_body(c, carry):
            rows = pl.ds(pl.multiple_of(c * ATT_TQ, ATT_TQ), ATT_TQ)
            o_ref[0, rows, :] = jnp.concatenate([of_ref[s, rows, :] for s in range(nslab)], axis=1).astype(BF16)
            lse_ref[0, rows, :] = lf_ref[rows, :]
            return carry

        lax.fori_loop(0, L // ATT_TQ, out_body, 0)


def dilated_attention(proj3, bias_g, gi, dil):
    B, L, _ = proj3.shape
    M = L // dil
    width = HEADS_PER_GROUP * ATTN_HEAD_DIM
    nslab = width // 128
    qb, kb, vb = PC_Q // width + gi, PC_K // width + gi, PC_V // width + gi
    scratch = [
        pltpu.VMEM((M + 2 * ATT_SPAN, width), BF16),
        pltpu.VMEM((M + 2 * ATT_SPAN, width), BF16),
    ]
    if dil > 1:
        scratch += [pltpu.VMEM((nslab, L, 128), F32)] * 4 + [pltpu.VMEM((L, 128), F32)]
    return pl.pallas_call(
        functools.partial(_attn_kernel, dil=dil),
        grid=(B,),
        in_specs=[
            pl.BlockSpec((1, L, width), lambda b: (b, 0, qb)),
            pl.BlockSpec((1, L, width), lambda b: (b, 0, kb)),
            pl.BlockSpec((1, L, width), lambda b: (b, 0, vb)),
            pl.BlockSpec((HEADS_PER_GROUP, ATT_TQ, ATT_TK), lambda b: (0, 0, 0)),
        ],
        out_specs=[
            pl.BlockSpec((1, L, width), lambda b: (b, 0, 0)),
            pl.BlockSpec((1, L, 128), lambda b: (b, 0, 0)),
        ],
        out_shape=[
            jax.ShapeDtypeStruct((B, L, width), BF16),
            jax.ShapeDtypeStruct((B, L, 128), F32),
        ],
        scratch_shapes=scratch,
        compiler_params=_cparams(1),
        name=f"dilated_attention_d{dil}",
    )(proj3, proj3, proj3, bias_g)


def _mix_kernel(x_ref, y_ref, z_ref, g_ref, o0_ref, o1_ref, o2_ref, l0_ref, l1_ref, l2_ref,
                nw_ref, wssd_ref, watt_ref, wout_ref, out_ref):
    z = z_ref[...].astype(F32)
    yz = y_ref[...].astype(F32) * (z * jax.nn.sigmoid(z))
    ms = jnp.mean(yz * yz, axis=-1, keepdims=True)
    yn = ((yz * lax.rsqrt(ms + RMS_EPS)) * nw_ref[...]).astype(BF16)
    pa = jnp.dot(yn, wssd_ref[...], preferred_element_type=F32)

    ls = [l0_ref[...], l1_ref[...], l2_ref[...]]
    mx = jnp.maximum(jnp.maximum(ls[0], ls[1]), ls[2])
    es = [jnp.exp(l - mx) for l in ls]
    den = es[0] + es[1] + es[2]
    erow = lax.broadcasted_iota(jnp.int32, (128, ATTN_OUT_WIDTH), 0)
    ecol = lax.broadcasted_iota(jnp.int32, (128, ATTN_OUT_WIDTH), 1)
    expand = (erow == ecol // ATTN_HEAD_DIM).astype(F32)
    yatt = None
    for e, o_ref in zip(es, (o0_ref, o1_ref, o2_ref)):
        wexp = jnp.dot(e / den, expand, precision=HIGHEST, preferred_element_type=F32)
        term = wexp * o_ref[...].astype(F32)
        yatt = term if yatt is None else yatt + term
    pb = jnp.dot(yatt.astype(BF16), watt_ref[...], preferred_element_type=F32)

    gates = jax.nn.sigmoid(g_ref[...].astype(F32))
    merged = gates[:, :D_MODEL] * pa + gates[:, D_MODEL:] * pb
    out_ref[...] = x_ref[...] + jnp.dot(merged.astype(BF16), wout_ref[...], preferred_element_type=F32)


def mix_project(x2d, yssd, proj, outs, lses, norm_w, w_ssd, w_att, w_out, tm=512):
    T = x2d.shape[0]
    row = lambda i: (i, 0)
    const = lambda i: (0, 0)
    return pl.pallas_call(
        _mix_kernel,
        grid=(T // tm,),
        in_specs=[
            pl.BlockSpec((tm, D_MODEL), row),
            pl.BlockSpec((tm, D_INNER), row),
            pl.BlockSpec((tm, D_INNER), lambda i: (i, PC_Z // D_INNER)),
            pl.BlockSpec((tm, 2 * D_MODEL), lambda i: (i, PC_GATE // (2 * D_MODEL))),
            pl.BlockSpec((tm, ATTN_OUT_WIDTH), row),
            pl.BlockSpec((tm, ATTN_OUT_WIDTH), row),
            pl.BlockSpec((tm, ATTN_OUT_WIDTH), row),
            pl.BlockSpec((tm, 128), row),
            pl.BlockSpec((tm, 128), row),
            pl.BlockSpec((tm, 128), row),
            pl.BlockSpec((1, D_INNER), const),
            pl.BlockSpec((D_INNER, D_MODEL), const),
            pl.BlockSpec((ATTN_OUT_WIDTH, D_MODEL), const),
            pl.BlockSpec((D_MODEL, D_MODEL), const),
        ],
        out_specs=pl.BlockSpec((tm, D_MODEL), row),
        out_shape=jax.ShapeDtypeStruct((T, D_MODEL), F32),
        compiler_params=_cparams(1),
        name="mix_project",
    )(x2d, yssd, proj, proj, outs[0], outs[1], outs[2], lses[0], lses[1], lses[2], norm_w, w_ssd, w_att, w_out)


ROUTE_LANES = 128
N_CAND = 80


def _extract_top(vals, ids, n, big):
    out_v, out_i = [], []
    for _ in range(n):
        m = jnp.max(vals, axis=0, keepdims=True)
        pick = jnp.min(jnp.where(vals == m, ids, big), axis=0, keepdims=True)
        out_v.append(m)
        out_i.append(pick)
        vals = jnp.where(ids == pick, -jnp.inf, vals)
    return out_v, out_i


def _extract_top_paired(vals, ids, n, big):
    half = vals.shape[0] // 2
    first = vals[:half] >= vals[half:]
    win_v = jnp.where(first, vals[:half], vals[half:])
    win_i = jnp.where(first, ids[:half], ids[half:])
    los_v = jnp.where(first, vals[half:], vals[:half])
    los_i = jnp.where(first, ids[half:], ids[:half])
    out_v, out_i = [], []
    for _ in range(n):
        m = jnp.max(win_v, axis=0, keepdims=True)
        pick = jnp.min(jnp.where(win_v == m, win_i, big), axis=0, keepdims=True)
        out_v.append(m)
        out_i.append(pick)
        hit = win_i == pick
        win_v = jnp.where(hit, los_v, win_v)
        win_i = jnp.where(hit, los_i, win_i)
        los_v = jnp.where(hit, -jnp.inf, los_v)
    return out_v, out_i


def _stack_rows(rows):
    n = len(rows)
    rid = lax.broadcasted_iota(jnp.int32, (n, rows[0].shape[1]), 0)
    out = jnp.broadcast_to(rows[0], rid.shape)
    for k in range(1, n):
        out = jnp.where(rid == k, rows[k], out)
    return out


def _peer_route_kernel(x_ref, nw_ref, wq_ref, keys_ref, xn_ref, idx_ref, g_ref, qt_ref, it_ref, gt_ref):
    tm = x_ref.shape[0]
    x = x_ref[...]
    ms = jnp.mean(x * x, axis=-1, keepdims=True)
    xn = ((x * lax.rsqrt(ms + RMS_EPS)) * nw_ref[...]).astype(BF16)
    xn_ref[...] = xn
    qt_ref[...] = lax.dot_general(wq_ref[...], xn, (((1,), (1,)), ((), ())),
                                  preferred_element_type=F32).astype(BF16)

    key_id = lax.broadcasted_iota(jnp.int32, (PEER_N_KEYS, ROUTE_LANES), 0).astype(F32)
    r80 = lax.broadcasted_iota(jnp.int32, (N_CAND, ROUTE_LANES), 0)
    rank_a = jnp.where(r80 < 16, 0, jnp.where(r80 < 72, 1 + lax.shift_right_logical(r80 - 16, 3), r80 - 64))
    rank_b = jnp.where(r80 < 16, r80, jnp.where(r80 < 72, (r80 - 16) & 7, 0))
    code = (rank_a * PEER_TOPK + rank_b).astype(F32)

    def route(h, lane0):
        lanes = slice(lane0, lane0 + ROUTE_LANES)
        sv, si = [], []
        for s in range(2):
            qhs = qt_ref[pl.ds(pl.multiple_of((2 * h + s) * PEER_HALF, PEER_HALF), PEER_HALF), lanes]
            sc = jnp.dot(keys_ref[s], qhs, preferred_element_type=F32)
            v, i = _extract_top_paired(sc, key_id, PEER_TOPK, float(PEER_N_KEYS))
            sv.append(v)
            si.append(i)
        sv1 = _stack_rows(sv[1])
        si1 = _stack_rows(si[1])
        cand = [sv[0][0] + sv1]
        eid = [si[0][0] * PEER_N_KEYS + si1]
        for a in range(1, 8):
            cand.append(sv[0][a] + sv1[0:8])
            eid.append(si[0][a] * PEER_N_KEYS + si1[0:8])
        cand.append(_stack_rows(sv[0][8:16]) + sv1[0:1])
        eid.append(_stack_rows(si[0][8:16]) * PEER_N_KEYS + si1[0:1])
        cand = jnp.concatenate(cand, axis=0)
        eid = jnp.concatenate(eid, axis=0)
        cv, cc = _extract_top(cand, code * PEER_EXPERTS + eid, PEER_TOPK, float(256 * PEER_EXPERTS))
        picked = [c_k - jnp.floor(c_k * (1.0 / PEER_EXPERTS)) * PEER_EXPERTS for c_k in cc]
        cvs = _stack_rows(cv)
        ex = jnp.exp(cvs - cvs[0:1])
        gate = ex / jnp.sum(ex, axis=0, keepdims=True)
        rows = pl.ds(pl.multiple_of(h * PEER_TOPK, PEER_TOPK), PEER_TOPK)
        it_ref[rows, lanes] = _stack_rows(picked)
        gt_ref[rows, lanes] = gate

    def body(h, carry):
        for lane0 in range(0, tm, ROUTE_LANES):
            route(h, lane0)
        return carry

    lax.fori_loop(0, PEER_HEADS, body, 0)
    idx_ref[...] = it_ref[...].T.astype(jnp.int32)
    g_ref[...] = gt_ref[...].T


def peer_route(x2d, norm_w, wq_t, keys, tm=256):
    T = x2d.shape[0]
    npick = PEER_HEADS * PEER_TOPK
    return pl.pallas_call(
        _peer_route_kernel,
        grid=(T // tm,),
        in_specs=[
            pl.BlockSpec((tm, D_MODEL), lambda i: (i, 0)),
            pl.BlockSpec((1, D_MODEL), lambda i: (0, 0)),
            pl.BlockSpec((2 * PEER_HEADS * PEER_HALF, D_MODEL), lambda i: (0, 0)),
            pl.BlockSpec((2, PEER_N_KEYS, PEER_HALF), lambda i: (0, 0, 0)),
        ],
        out_specs=[
            pl.BlockSpec((tm, D_MODEL), lambda i: (i, 0)),
            pl.BlockSpec((tm, npick), lambda i: (i, 0)),
            pl.BlockSpec((tm, npick), lambda i: (i, 0)),
        ],
        out_shape=[
            jax.ShapeDtypeStruct((T, D_MODEL), BF16),
            jax.ShapeDtypeStruct((T, npick), jnp.int32),
            jax.ShapeDtypeStruct((T, npick), F32),
        ],
        scratch_shapes=[
            pltpu.VMEM((2 * PEER_HEADS * PEER_HALF, tm), BF16),
            pltpu.VMEM((npick, tm), F32),
            pltpu.VMEM((npick, tm), F32),
        ],
        compiler_params=_cparams(1),
        name="peer_route",
    )(x2d, norm_w, wq_t, keys)


GATE_PITCH = 136
GATE_UNROLL = 32


def _peer_gates_kernel(idx_ref, g_ref, gd_ref, g3_ref):
    tm = idx_ref.shape[0]
    rid = lax.broadcasted_iota(jnp.int32, (PEER_N_KEYS, 128), 0)

    def body(tb, carry):
        t0 = pl.multiple_of(tb * GATE_UNROLL, GATE_UNROLL)
        e = idx_ref[pl.ds(t0, GATE_UNROLL), :]
        g = g_ref[pl.ds(t0, GATE_UNROLL), :]
        i1 = lax.shift_right_logical(e, 7)
        i2 = e & (PEER_N_KEYS - 1)
        for u in range(GATE_UNROLL):
            a_t = jnp.where(rid == i1[u:u + 1], g[u:u + 1], 0.0).astype(BF16)
            b_t = jnp.where(rid == i2[u:u + 1], 1.0, 0.0).astype(BF16)
            gd = lax.dot_general(a_t, b_t, (((1,), (1,)), ((), ())), preferred_element_type=F32)
            g3_ref[pl.ds(t0 + u, PEER_N_KEYS, stride=GATE_PITCH), :] = gd
        return carry

    lax.fori_loop(0, tm // GATE_UNROLL, body, 0)
    for a in range(PEER_N_KEYS):
        gd_ref[:, a * PEER_N_KEYS:(a + 1) * PEER_N_KEYS] = g3_ref[a * GATE_PITCH:a * GATE_PITCH + tm, :].astype(BF16)


def peer_gates(idx, g, tm=128):
    T, npick = idx.shape
    return pl.pallas_call(
        _peer_gates_kernel,
        grid=(T // tm,),
        in_specs=[pl.BlockSpec((tm, npick), lambda i: (i, 0)), pl.BlockSpec((tm, npick), lambda i: (i, 0))],
        out_specs=pl.BlockSpec((tm, PEER_EXPERTS), lambda i: (i, 0)),
        out_shape=jax.ShapeDtypeStruct((T, PEER_EXPERTS), BF16),
        scratch_shapes=[pltpu.VMEM((PEER_N_KEYS * GATE_PITCH, PEER_N_KEYS), F32)],
        compiler_params=_cparams(1),
        name="peer_gates",
    )(idx, g)


def _peer_experts_kernel(x_ref, xn_ref, gd_ref, ut_ref, v_ref, fw_ref, out_ref, acc_ref, *, final_norm):
    j = pl.program_id(1)

    @pl.when(j == 0)
    def _():
        acc_ref[...] = jnp.zeros_like(acc_ref)

    z = jnp.dot(xn_ref[...], ut_ref[...], preferred_element_type=F32)
    hid = 0.5 * z * (1.0 + lax.erf(z * np.float32(np.sqrt(0.5))))
    w = (gd_ref[...].astype(F32) * hid).astype(BF16)
    acc_ref[...] += jnp.dot(w, v_ref[...], preferred_element_type=F32)

    @pl.when(j == pl.num_programs(1) - 1)
    def _():
        y = x_ref[...] + acc_ref[...]
        if final_norm:
            ms = jnp.mean(y * y, axis=-1, keepdims=True)
            y = (y * lax.rsqrt(ms + RMS_EPS)) * fw_ref[...]
        out_ref[...] = y


def peer_experts(x2d, xn, gd, u_t, v, final_w, final_norm, tm=512, te=1024):
    T = x2d.shape[0]
    return pl.pallas_call(
        functools.partial(_peer_experts_kernel, final_norm=final_norm),
        grid=(T // tm, PEER_EXPERTS // te),
        in_specs=[
            pl.BlockSpec((tm, D_MODEL), lambda i, j: (i, 0)),
            pl.BlockSpec((tm, D_MODEL), lambda i, j: (i, 0)),
            pl.BlockSpec((tm, te), lambda i, j: (i, j)),
            pl.BlockSpec((D_MODEL, te), lambda i, j: (0, j)),
            pl.BlockSpec((te, D_MODEL), lambda i, j: (j, 0)),
            pl.BlockSpec((1, D_MODEL), lambda i, j: (0, 0)),
        ],
        out_specs=pl.BlockSpec((tm, D_MODEL), lambda i, j: (i, 0)),
        out_shape=jax.ShapeDtypeStruct((T, D_MODEL), F32),
        scratch_shapes=[pltpu.VMEM((tm, D_MODEL), F32)],
        compiler_params=_cparams(2),
        name="peer_experts_final" if final_norm else "peer_experts",
    )(x2d, xn, gd, u_t, v, final_w)


def _t5_bucket(rel):
    half = REL_BUCKETS // 2
    max_exact = half // 2
    n = np.abs(rel)
    large = max_exact + (np.log(np.maximum(n, 1) / max_exact) / np.log(REL_MAX_DISTANCE / max_exact)
                         * (half - max_exact)).astype(np.int32)
    large = np.minimum(large, half - 1)
    return np.where(rel > 0, half, 0) + np.where(n < max_exact, n, large)


def prep_attn_bias(rel_bias, gi, dil):
    i = np.arange(ATT_TQ)[:, None]
    j = np.arange(ATT_TK)[None, :]
    rel = j - ATT_SPAN - i
    bucket = _t5_bucket(rel * dil)
    table = rel_bias[bucket][:, :, gi * HEADS_PER_GROUP:(gi + 1) * HEADS_PER_GROUP]
    table = jnp.where((np.abs(rel) <= ATT_SPAN)[:, :, None], table.astype(F32), NEG)
    return jnp.transpose(table, (2, 0, 1))


def _dt_col_index():
    g = np.arange(SSD_GROUPS)[:, None, None, None]
    d = np.arange(2)[None, :, None, None]
    h = np.arange(4)[None, None, :, None]
    rep = np.zeros((1, 1, 1, DT_REP), np.int64)
    return (d * SSD_HEADS + 4 * g + h + rep).reshape(-1)


def prep_w_in(w_in_l):
    w_main = jnp.concatenate(
        [w_in_l[:, :OFF_XBC], w_in_l[:, OFF_GATE:], w_in_l[:, OFF_XBC:OFF_DT], w_in_l[:, OFF_Q:OFF_GATE]], axis=1
    ).astype(BF16)
    w_dtx = w_in_l[:, OFF_DT + _dt_col_index()].astype(BF16)
    return w_main, w_dtx


def prep_ssd_params(dt_bias_l, a_log_l, d_skip_l):
    idx = _dt_col_index()
    dtb_g = dt_bias_l.reshape(-1)[idx].reshape(SSD_GROUPS, 1, 128)
    alog_g = a_log_l.reshape(-1)[idx].reshape(SSD_GROUPS, 1, 128)
    dsk_g = jnp.repeat(d_skip_l, SSD_HEAD_DIM).reshape(SSD_GROUPS, 1, 256)
    return dtb_g, alog_g, dsk_g


def trunk_layer(x2d, B, L, biases, norm1_w, w_in, conv_w, conv_b, a_log, dt_bias, d_skip, ssd_norm_w,
                w_ssd_out, w_attn_out, w_out, norm2_w, w_query, sub_keys, expert_u, expert_v, final_w, final_norm):
    w_main, w_dtx = prep_w_in(w_in)
    proj, dtx = in_proj(x2d, norm1_w.reshape(1, -1), w_main, w_dtx)
    proj3 = proj.reshape(B, L, PROJ_COLS)
    yssd = ssd_mixer(proj3, dtx.reshape(B, L, DTX_COLS), conv_w, conv_b.reshape(1, -1),
                     *prep_ssd_params(dt_bias, a_log, d_skip))
    outs, lses = [], []
    for gi, (_, dil) in enumerate(DILATION_GROUPS):
        o, lse = dilated_attention(proj3, biases[gi], gi, dil)
        outs.append(o.reshape(B * L, ATTN_OUT_WIDTH))
        lses.append(lse.reshape(B * L, 128))
    x1 = mix_project(x2d, yssd.reshape(B * L, D_INNER), proj, outs, lses, ssd_norm_w.reshape(1, -1),
                     w_ssd_out.astype(BF16), w_attn_out.astype(BF16), w_out.astype(BF16))
    xn, idx, gate = peer_route(x1, norm2_w.reshape(1, -1), w_query.T.astype(BF16), sub_keys.astype(BF16))
    gd = peer_gates(idx, gate)
    return peer_experts(x1, xn, gd, expert_u.T.astype(BF16), expert_v.astype(BF16), final_w.reshape(1, -1), final_norm)


def kernel(x_prompt, x_sample, rel_bias, norm1_w, w_in, conv_w, conv_b, A_log, dt_bias, D_skip, ssd_norm_w,
           w_ssd_out, w_attn_out, w_out, norm2_w, w_query, sub_keys, expert_u, expert_v, final_norm_w):
    nb_prompt = x_prompt.shape[0]
    x = jnp.concatenate([x_prompt, x_sample], axis=0)
    B, L, _ = x.shape
    x2d = x.reshape(B * L, D_MODEL)
    biases = [prep_attn_bias(rel_bias, gi, dil) for gi, (_, dil) in enumerate(DILATION_GROUPS)]
    for l in range(DEPTH):
        x2d = trunk_layer(x2d, B, L, biases, norm1_w[l], w_in[l], conv_w[l], conv_b[l], A_log[l], dt_bias[l],
                          D_skip[l], ssd_norm_w[l], w_ssd_out[l], w_attn_out[l], w_out[l], norm2_w[l], w_query[l],
                          sub_keys[l], expert_u[l], expert_v[l], final_norm_w, l == DEPTH - 1)
    y = x2d.reshape(B, L, D_MODEL)
    return y[:nb_prompt], y[nb_prompt:]
```

```python
import functools

import numpy as np
import jax
import jax.numpy as jnp
from jax import lax
from jax.experimental import pallas as pl
from jax.experimental.pallas import tpu as pltpu

F32 = jnp.float32
BF16 = jnp.bfloat16
HIGHEST = lax.Precision.HIGHEST

D_MODEL = 1024
DEPTH = 2
D_INNER = 2048
SSD_HEADS = 32
SSD_HEAD_DIM = 64
SSD_GROUPS = 8
SSD_STATE = 128
CHUNK = 128
D_CONV = 5
GN = SSD_GROUPS * SSD_STATE
CONV_DIM = D_INNER + 2 * GN
ATTN_HEAD_DIM = 64
HEADS_PER_GROUP = 8
DILATION_GROUPS = ((128, 1), (512, 4), (2048, 16))
ATTN_WIDTH = 1536
ATTN_OUT_WIDTH = 512
REL_BUCKETS = 32
REL_MAX_DISTANCE = 1024
PEER_HEADS = 8
PEER_N_KEYS = 128
PEER_EXPERTS = PEER_N_KEYS * PEER_N_KEYS
PEER_HALF = 128
PEER_TOPK = 16
RMS_EPS = 1e-6

OFF_XBC = D_INNER
OFF_DT = OFF_XBC + CONV_DIM
OFF_Q = OFF_DT + 2 * SSD_HEADS

OFF_K = OFF_Q + ATTN_WIDTH
OFF_V = OFF_K + ATTN_WIDTH
OFF_GATE = OFF_V + ATTN_WIDTH

PC_Z = 0
PC_GATE = D_INNER
PC_XBC = PC_GATE + 2 * D_MODEL
PC_Q = PC_XBC + CONV_DIM
PC_K = PC_Q + ATTN_WIDTH
PC_V = PC_K + ATTN_WIDTH
PROJ_COLS = PC_V + ATTN_WIDTH
DT_REP = 16
DTX_COLS = SSD_GROUPS * 128

ATT_SPAN = 64
ATT_TQ = 128
ATT_TK = 256
NEG = -1e30

VMEM_LIMIT = 56 * 1024 * 1024


def _cparams(n_axes):
    return pltpu.CompilerParams(
        dimension_semantics=("arbitrary",) * n_axes, vmem_limit_bytes=VMEM_LIMIT
    )


def _inproj_kernel(x_ref, nw_ref, w_ref, wdt_ref, o_ref, dt_ref, xn_ref):
    @pl.when(pl.program_id(1) == 0)
    def _():
        x = x_ref[...]
        ms = jnp.mean(x * x, axis=-1, keepdims=True)
        xn = ((x * lax.rsqrt(ms + RMS_EPS)) * nw_ref[...]).astype(BF16)
        xn_ref[...] = xn
        dt_ref[...] = jnp.dot(xn, wdt_ref[...], preferred_element_type=F32)

    o_ref[...] = jnp.dot(xn_ref[...], w_ref[...], preferred_element_type=F32).astype(BF16)


def in_proj(x2d, norm_w, w_main, w_dtx, tm=1024, tn=1280):
    T = x2d.shape[0]
    nc = w_main.shape[1]
    return pl.pallas_call(
        _inproj_kernel,
        grid=(T // tm, nc // tn),
        in_specs=[
            pl.BlockSpec((tm, D_MODEL), lambda i, j: (i, 0)),
            pl.BlockSpec((1, D_MODEL), lambda i, j: (0, 0)),
            pl.BlockSpec((D_MODEL, tn), lambda i, j: (0, j)),
            pl.BlockSpec((D_MODEL, DTX_COLS), lambda i, j: (0, 0)),
        ],
        out_specs=[
            pl.BlockSpec((tm, tn), lambda i, j: (i, j)),
            pl.BlockSpec((tm, DTX_COLS), lambda i, j: (i, 0)),
        ],
        out_shape=[
            jax.ShapeDtypeStruct((T, nc), BF16),
            jax.ShapeDtypeStruct((T, DTX_COLS), F32),
        ],
        scratch_shapes=[pltpu.VMEM((tm, D_MODEL), BF16)],
        compiler_params=_cparams(2),
        name="in_proj",
    )(x2d, norm_w, w_main, w_dtx)


SSD_HALO = 64


def _split3(x):
    hi = x.astype(BF16).astype(F32)
    r1 = x - hi
    mid = r1.astype(BF16).astype(F32)
    lo = (r1 - mid).astype(BF16).astype(F32)
    return hi, mid, lo


def _ssd_kernel(x_ref, b_ref, c_ref, dt_ref, cwx_ref, cwb_ref, cwc_ref, cbx_ref, cbb_ref, cbc_ref,
                dtb_ref, alog_ref, dsk_ref, y_ref,
                pad_ref, xs_ref, bt_ref, cm_ref, yacc_ref, st_ref, pk_ref, ng_ref):
    L = x_ref.shape[1]
    nch = L // CHUNK
    cw = jnp.concatenate([cwx_ref[...], cwb_ref[...], cwc_ref[...]], axis=1)
    cb = jnp.concatenate([cbx_ref[...], cbb_ref[...], cbc_ref[...]], axis=1)
    dsk = dsk_ref[0]

    zhalo = jnp.zeros((SSD_HALO, 512), BF16)
    pad_ref[0:SSD_HALO, :] = zhalo
    pad_ref[L + SSD_HALO:L + 2 * SSD_HALO, :] = zhalo

    def copy_body(c, carry):
        r = pl.multiple_of(c * CHUNK, CHUNK)
        dst = pl.ds(r + SSD_HALO, CHUNK)
        pad_ref[dst, 0:256] = x_ref[0, pl.ds(r, CHUNK), :]
        pad_ref[dst, 256:384] = b_ref[0, pl.ds(r, CHUNK), :]
        pad_ref[dst, 384:512] = c_ref[0, pl.ds(r, CHUNK), :]
        return carry

    lax.fori_loop(0, nch, copy_body, 0)

    taps = (-2, -1, 1, 2)
    srow = lax.broadcasted_iota(jnp.int32, (len(taps) * CHUNK, CHUNK + 2 * SSD_HALO), 0)
    scol = lax.broadcasted_iota(jnp.int32, (len(taps) * CHUNK, CHUNK + 2 * SSD_HALO), 1)
    soff = jnp.where(srow < CHUNK, taps[0], jnp.where(srow < 2 * CHUNK, taps[1], jnp.where(srow < 3 * CHUNK, taps[2], taps[3])))
    shift_mat = (scol == (srow & (CHUNK - 1)) + SSD_HALO + soff).astype(BF16)

    def conv_body(c, carry):
        r = pl.multiple_of(c * CHUNK, CHUNK)
        win = pad_ref[pl.ds(r, CHUNK + 2 * SSD_HALO), :]
        sh = jnp.dot(shift_mat, win, preferred_element_type=F32)
        acc = win[SSD_HALO:SSD_HALO + CHUNK, :].astype(F32) * cw[2:3, :] + cb
        for i, w in enumerate((0, 1, 3, 4)):
            acc = acc + sh[i * CHUNK:(i + 1) * CHUNK, :] * cw[w:w + 1, :]
        act = acc * jax.nn.sigmoid(acc)
        xs = act[:, 0:256]
        xs_ref[pl.ds(r, CHUNK), :] = xs
        yacc_ref[pl.ds(r, CHUNK), :] = xs * dsk
        bt_ref[c] = act[:, 256:384].T.astype(BF16)
        cm_ref[pl.ds(r, CHUNK), :] = act[:, 384:512].astype(BF16)
        return carry

    lax.fori_loop(0, nch, conv_body, 0, unroll=2)

    st_ref[...] = jnp.zeros_like(st_ref)

    row = lax.broadcasted_iota(jnp.int32, (CHUNK, CHUNK), 0)
    col = lax.broadcasted_iota(jnp.int32, (CHUNK, CHUNK), 1)
    lane_lt64 = col < 64
    m16 = col & (DT_REP - 1)
    a_neg = -jnp.exp(alog_ref[0])
    dtb = dtb_ref[0]
    erow = lax.broadcasted_iota(jnp.int32, (CHUNK, 2 * CHUNK), 0)
    ecol = lax.broadcasted_iota(jnp.int32, (CHUNK, 2 * CHUNK), 1)

    def selectors(d):
        rel = erow - (64 * d + DT_REP * lax.shift_right_logical(ecol, 6))
        sel_cs = ((rel >= 0) & (rel < 3)).astype(BF16)
        sel_dt = ((rel >= 6) & (rel < 9)).astype(BF16)
        return sel_cs, sel_dt

    sels = (selectors(0), selectors(1))
    tri3 = tuple(jnp.concatenate([m.astype(BF16)] * 3, axis=1) for m in (row >= col, row <= col))

    head_rows = [((row >= DT_REP * k) & (row < DT_REP * k + 6)).astype(BF16) for k in range(8)]

    def prep_body(c, carry):
        rows = pl.ds(pl.multiple_of(c * CHUNK, CHUNK), CHUNK)
        dt = jax.nn.softplus(dt_ref[0, rows, :] + dtb)
        a = dt * a_neg
        a3 = jnp.concatenate([t.astype(BF16) for t in _split3(a)], axis=0)
        cs = jnp.where(lane_lt64,
                       jnp.dot(tri3[0], a3, preferred_element_type=F32),
                       jnp.dot(tri3[1], a3, preferred_element_type=F32))
        c_hi, c_mid, c_lo = _split3(cs)
        d_hi, d_mid, d_lo = _split3(dt)
        packed = jnp.where(m16 == 0, c_hi, jnp.where(m16 == 1, c_mid, jnp.where(m16 == 2, c_lo, jnp.where(
            m16 < 6, 1.0, jnp.where(m16 == 6, d_hi, jnp.where(m16 == 7, d_mid, jnp.where(m16 == 8, d_lo, 0.0)))))))
        pk_ref[c] = packed.astype(BF16)
        negs = jnp.where(m16 < 3, 1.0, jnp.where(m16 == 3, -c_hi, jnp.where(m16 == 4, -c_mid, jnp.where(
            m16 == 5, -c_lo, 0.0))))
        negs_t = negs.T.astype(BF16)
        for d in range(2):
            ng_ref[c, d] = jnp.concatenate([negs_t * head_rows[4 * d + h] for h in range(4)], axis=1)
        return carry

    lax.fori_loop(0, nch, prep_body, 0, unroll=2)

    half_lo = lane_lt64.astype(BF16)
    half_hi = 1 - half_lo

    def one_pass(d, c):
        r = pl.multiple_of(c * CHUNK, CHUNK)
        rows = pl.ds(r, CHUNK)
        mask = (row >= col) if d == 0 else (row <= col)
        last = CHUNK - 1 if d == 0 else 0
        packed = pk_ref[c]
        sel_cs, sel_dt = sels[d]
        cs2 = jnp.dot(packed, sel_cs, preferred_element_type=F32)
        dt2 = jnp.dot(packed, sel_dt, preferred_element_type=F32)
        diff = jnp.dot(packed, ng_ref[c, d], preferred_element_type=F32)
        tot2 = cs2[last:last + 1, :]
        ein2 = jnp.exp(cs2)
        dd2 = jnp.exp(tot2 - cs2) * dt2
        cd2 = jnp.exp(tot2)
        cmat = cm_ref[rows, :]
        btc = bt_ref[c]
        cbm = jnp.dot(cmat, btc, preferred_element_type=F32)
        xs = xs_ref[rows, :]
        xg = (xs * dt2).astype(BF16)
        xdec = (xs * dd2).astype(BF16)
        ydiag = []
        for p in range(2):
            cols = slice(128 * p, 128 * p + 128)
            gs = []
            for hh in range(2):
                h = 2 * p + hh
                seg = jnp.where(mask, jnp.exp(diff[:, 128 * h:128 * h + 128]), 0.0)
                gs.append((cbm * seg).astype(BF16))
            x2 = xg[:, cols]
            ydiag.append(jnp.dot(jnp.concatenate(gs, axis=1), jnp.concatenate([x2 * half_lo, x2 * half_hi], axis=0),
                                 preferred_element_type=F32))
        st = st_ref[d]
        yoff = jnp.dot(cmat, st.astype(BF16), preferred_element_type=F32) * ein2
        yacc_ref[rows, :] += jnp.concatenate(ydiag, axis=1) + yoff
        st_ref[d] = st * cd2 + jnp.dot(btc, xdec, preferred_element_type=F32)

    def main_body(i, carry):
        one_pass(0, 2 * i)
        one_pass(1, nch - 1 - 2 * i)
        one_pass(0, 2 * i + 1)
        one_pass(1, nch - 2 - 2 * i)
        return carry

    lax.fori_loop(0, nch // 2, main_body, 0)

    def out_body(c, carry):
        rows = pl.ds(pl.multiple_of(c * CHUNK, CHUNK), CHUNK)
        y_ref[0, rows, :] = yacc_ref[rows, :].astype(BF16)
        return carry

    lax.fori_loop(0, nch, out_body, 0)


def ssd_mixer(proj3, dtx3, conv_w, conv_b, dtb_g, alog_g, dsk_g):
    B, L, _ = proj3.shape
    xb = PC_XBC // 256
    bb = (PC_XBC + D_INNER) // 128
    cbk = (PC_XBC + D_INNER + GN) // 128
    return pl.pallas_call(
        _ssd_kernel,
        grid=(B, SSD_GROUPS),
        in_specs=[
            pl.BlockSpec((1, L, 256), lambda b, g: (b, 0, xb + g)),
            pl.BlockSpec((1, L, 128), lambda b, g: (b, 0, bb + g)),
            pl.BlockSpec((1, L, 128), lambda b, g: (b, 0, cbk + g)),
            pl.BlockSpec((1, L, 128), lambda b, g: (b, 0, g)),
            pl.BlockSpec((D_CONV, 256), lambda b, g: (0, g)),
            pl.BlockSpec((D_CONV, 128), lambda b, g: (0, D_INNER // 128 + g)),
            pl.BlockSpec((D_CONV, 128), lambda b, g: (0, (D_INNER + GN) // 128 + g)),
            pl.BlockSpec((1, 256), lambda b, g: (0, g)),
            pl.BlockSpec((1, 128), lambda b, g: (0, D_INNER // 128 + g)),
            pl.BlockSpec((1, 128), lambda b, g: (0, (D_INNER + GN) // 128 + g)),
            pl.BlockSpec((1, 1, 128), lambda b, g: (g, 0, 0)),
            pl.BlockSpec((1, 1, 128), lambda b, g: (g, 0, 0)),
            pl.BlockSpec((1, 1, 256), lambda b, g: (g, 0, 0)),
        ],
        out_specs=pl.BlockSpec((1, L, 256), lambda b, g: (b, 0, g)),
        out_shape=jax.ShapeDtypeStruct((B, L, D_INNER), BF16),
        scratch_shapes=[
            pltpu.VMEM((L + 2 * SSD_HALO, 512), BF16),
            pltpu.VMEM((L, 256), F32),
            pltpu.VMEM((L // CHUNK, SSD_STATE, CHUNK), BF16),
            pltpu.VMEM((L, SSD_STATE), BF16),
            pltpu.VMEM((L, 256), F32),
            pltpu.VMEM((2, SSD_STATE, 256), F32),
            pltpu.VMEM((L // CHUNK, CHUNK, 128), BF16),
            pltpu.VMEM((L // CHUNK, 2, 128, 4 * CHUNK), BF16),
        ],
        compiler_params=_cparams(2),
        name="ssd_mixer",
    )(proj3, proj3, proj3, dtx3, conv_w, conv_w, conv_w, conv_b, conv_b, conv_b, dtb_g, alog_g, dsk_g)


def _attn_kernel(q_ref, k_ref, v_ref, bias_ref, o_ref, lse_ref, kp_ref, vp_ref, *stage, dil):
    L = q_ref.shape[1]
    M = L // dil
    nt = M // ATT_TQ
    width = q_ref.shape[2]
    nslab = width // 128
    zpad = jnp.zeros((ATT_SPAN, width), BF16)
    kp_ref[0:ATT_SPAN, :] = zpad
    kp_ref[M + ATT_SPAN:M + 2 * ATT_SPAN, :] = zpad
    vp_ref[0:ATT_SPAN, :] = zpad
    vp_ref[M + ATT_SPAN:M + 2 * ATT_SPAN, :] = zpad

    if dil > 1:
        qf_ref, kf_ref, vf_ref, of_ref, lf_ref = stage

        def stage_body(c, carry):
            rows = pl.ds(pl.multiple_of(c * ATT_TQ, ATT_TQ), ATT_TQ)
            for s in range(nslab):
                cols = slice(128 * s, 128 * s + 128)
                qf_ref[s, rows, :] = q_ref[0, rows, cols].astype(F32)
                kf_ref[s, rows, :] = k_ref[0, rows, cols].astype(F32)
                vf_ref[s, rows, :] = v_ref[0, rows, cols].astype(F32)
            return carry

        lax.fori_loop(0, L // ATT_TQ, stage_body, 0)
    else:
        qf_ref = kf_ref = vf_ref = of_ref = lf_ref = None

    def sub_rows(r, t):
        return pl.ds(r + t * (ATT_TQ * dil), ATT_TQ, stride=dil)

    def load_tile(ref, ref_f32, r, t):
        if dil == 1:
            return ref[0, pl.ds(pl.multiple_of(t * ATT_TQ, ATT_TQ), ATT_TQ), :]
        return jnp.concatenate([ref_f32[s, sub_rows(r, t), :] for s in range(nslab)], axis=1).astype(BF16)

    lane = lax.broadcasted_iota(jnp.int32, (ATT_TQ, 128), 1)
    lane_lt64 = lane < 64
    jj = lax.broadcasted_iota(jnp.int32, (ATT_TQ, ATT_TK), 1)

    def residue_body(r, carry):
        def copy_body(t, c2):
            dst = pl.ds(pl.multiple_of(t * ATT_TQ, ATT_TQ) + ATT_SPAN, ATT_TQ)
            kp_ref[dst, :] = load_tile(k_ref, kf_ref, r, t)
            vp_ref[dst, :] = load_tile(v_ref, vf_ref, r, t)
            return c2

        lax.fori_loop(0, nt, copy_body, 0)

        def tile_body(t, c2):
            r0 = pl.multiple_of(t * ATT_TQ, ATT_TQ)
            q = load_tile(q_ref, qf_ref, r, t) * jnp.asarray(ATTN_HEAD_DIM ** -0.5, BF16)
            kw = kp_ref[pl.ds(r0, ATT_TK), :]
            vw = vp_ref[pl.ds(r0, ATT_TK), :]
            in_seq = (jj >= ATT_SPAN - r0) & (jj < M + ATT_SPAN - r0)
            lse_tile = jnp.zeros((ATT_TQ, 128), F32)
            outs = []
            for p in range(HEADS_PER_GROUP // 2):
                cols = slice(128 * p, 128 * p + 128)
                q2, k2, v2 = q[:, cols], kw[:, cols], vw[:, cols]
                pv, dens = [], []
                for hh in range(2):
                    h = 2 * p + hh
                    qm = jnp.where(lane_lt64 if hh == 0 else jnp.logical_not(lane_lt64), q2, jnp.zeros_like(q2))
                    s = lax.dot_general(qm, k2, (((1,), (1,)), ((), ())), preferred_element_type=F32)
                    s = jnp.where(in_seq, s + bias_ref[h], NEG)
                    m = jnp.max(s, axis=-1, keepdims=True)
                    pexp = jnp.exp(s - m)
                    den = jnp.sum(pexp, axis=-1, keepdims=True)
                    pv.append(jnp.dot(pexp.astype(BF16), v2, preferred_element_type=F32))
                    dens.append(den)
                    lse_tile = jnp.where(lane == h, m + jnp.log(den), lse_tile)
                outs.append(jnp.where(lane_lt64, pv[0], pv[1]) / jnp.where(lane_lt64, dens[0], dens[1]))
            if dil == 1:
                o_ref[0, pl.ds(r0, ATT_TQ), :] = jnp.concatenate(outs, axis=1).astype(BF16)
                lse_ref[0, pl.ds(r0, ATT_TQ), :] = lse_tile
            else:
                for p, o2 in enumerate(outs):
                    of_ref[p, sub_rows(r, t), :] = o2
                lf_ref[sub_rows(r, t), :] = lse_tile
            return c2

        lax.fori_loop(0, nt, tile_body, 0)
        return carry

    if dil == 1:
        residue_body(0, 0)
    else:
        lax.fori_loop(0, dil, residue_body, 0)

        def out_body(c, carry):
            rows = pl.ds(pl.multiple_of(c * ATT_TQ, ATT_TQ), ATT_TQ)
            o_ref[0, rows, :] = jnp.concatenate([of_ref[s, rows, :] for s in range(nslab)], axis=1).astype(BF16)
            lse_ref[0, rows, :] = lf_ref[rows, :]
            return carry

        lax.fori_loop(0, L // ATT_TQ, out_body, 0)


def dilated_attention(proj3, bias_g, gi, dil):
    B, L, _ = proj3.shape
    M = L // dil
    width = HEADS_PER_GROUP * ATTN_HEAD_DIM
    nslab = width // 128
    qb, kb, vb = PC_Q // width + gi, PC_K // width + gi, PC_V // width + gi
    scratch = [
        pltpu.VMEM((M + 2 * ATT_SPAN, width), BF16),
        pltpu.VMEM((M + 2 * ATT_SPAN, width), BF16),
    ]
    if dil > 1:
        scratch += [pltpu.VMEM((nslab, L, 128), F32)] * 4 + [pltpu.VMEM((L, 128), F32)]
    return pl.pallas_call(
        functools.partial(_attn_kernel, dil=dil),
        grid=(B,),
        in_specs=[
            pl.BlockSpec((1, L, width), lambda b: (b, 0, qb)),
            pl.BlockSpec((1, L, width), lambda b: (b, 0, kb)),
            pl.BlockSpec((1, L, width), lambda b: (b, 0, vb)),
            pl.BlockSpec((HEADS_PER_GROUP, ATT_TQ, ATT_TK), lambda b: (0, 0, 0)),
        ],
        out_specs=[
            pl.BlockSpec((1, L, width), lambda b: (b, 0, 0)),
            pl.BlockSpec((1, L, 128), lambda b: (b, 0, 0)),
        ],
        out_shape=[
            jax.ShapeDtypeStruct((B, L, width), BF16),
            jax.ShapeDtypeStruct((B, L, 128), F32),
        ],
        scratch_shapes=scratch,
        compiler_params=_cparams(1),
        name=f"dilated_attention_d{dil}",
    )(proj3, proj3, proj3, bias_g)


def _mix_kernel(x_ref, y_ref, z_ref, g_ref, o0_ref, o1_ref, o2_ref, l0_ref, l1_ref, l2_ref,
                nw_ref, wssd_ref, watt_ref, wout_ref, out_ref):
    z = z_ref[...].astype(F32)
    yz = y_ref[...].astype(F32) * (z * jax.nn.sigmoid(z))
    ms = jnp.mean(yz * yz, axis=-1, keepdims=True)
    yn = ((yz * lax.rsqrt(ms + RMS_EPS)) * nw_ref[...]).astype(BF16)
    pa = jnp.dot(yn, wssd_ref[...], preferred_element_type=F32)

    ls = [l0_ref[...], l1_ref[...], l2_ref[...]]
    mx = jnp.maximum(jnp.maximum(ls[0], ls[1]), ls[2])
    es = [jnp.exp(l - mx) for l in ls]
    den = es[0] + es[1] + es[2]
    erow = lax.broadcasted_iota(jnp.int32, (128, ATTN_OUT_WIDTH), 0)
    ecol = lax.broadcasted_iota(jnp.int32, (128, ATTN_OUT_WIDTH), 1)
    expand = (erow == ecol // ATTN_HEAD_DIM).astype(F32)
    yatt = None
    for e, o_ref in zip(es, (o0_ref, o1_ref, o2_ref)):
        wexp = jnp.dot(e / den, expand, precision=HIGHEST, preferred_element_type=F32)
        term = wexp * o_ref[...].astype(F32)
        yatt = term if yatt is None else yatt + term
    pb = jnp.dot(yatt.astype(BF16), watt_ref[...], preferred_element_type=F32)

    gates = jax.nn.sigmoid(g_ref[...].astype(F32))
    merged = gates[:, :D_MODEL] * pa + gates[:, D_MODEL:] * pb
    out_ref[...] = x_ref[...] + jnp.dot(merged.astype(BF16), wout_ref[...], preferred_element_type=F32)


def mix_project(x2d, yssd, proj, outs, lses, norm_w, w_ssd, w_att, w_out, tm=512):
    T = x2d.shape[0]
    row = lambda i: (i, 0)
    const = lambda i: (0, 0)
    return pl.pallas_call(
        _mix_kernel,
        grid=(T // tm,),
        in_specs=[
            pl.BlockSpec((tm, D_MODEL), row),
            pl.BlockSpec((tm, D_INNER), row),
            pl.BlockSpec((tm, D_INNER), lambda i: (i, PC_Z // D_INNER)),
            pl.BlockSpec((tm, 2 * D_MODEL), lambda i: (i, PC_GATE // (2 * D_MODEL))),
            pl.BlockSpec((tm, ATTN_OUT_WIDTH), row),
            pl.BlockSpec((tm, ATTN_OUT_WIDTH), row),
            pl.BlockSpec((tm, ATTN_OUT_WIDTH), row),
            pl.BlockSpec((tm, 128), row),
            pl.BlockSpec((tm, 128), row),
            pl.BlockSpec((tm, 128), row),
            pl.BlockSpec((1, D_INNER), const),
            pl.BlockSpec((D_INNER, D_MODEL), const),
            pl.BlockSpec((ATTN_OUT_WIDTH, D_MODEL), const),
            pl.BlockSpec((D_MODEL, D_MODEL), const),
        ],
        out_specs=pl.BlockSpec((tm, D_MODEL), row),
        out_shape=jax.ShapeDtypeStruct((T, D_MODEL), F32),
        compiler_params=_cparams(1),
        name="mix_project",
    )(x2d, yssd, proj, proj, outs[0], outs[1], outs[2], lses[0], lses[1], lses[2], norm_w, w_ssd, w_att, w_out)


ROUTE_LANES = 128
N_CAND = 80


def _extract_top(vals, ids, n, big):
    out_v, out_i = [], []
    for _ in range(n):
        m = jnp.max(vals, axis=0, keepdims=True)
        pick = jnp.min(jnp.where(vals == m, ids, big), axis=0, keepdims=True)
        out_v.append(m)
        out_i.append(pick)
        vals = jnp.where(ids == pick, -jnp.inf, vals)
    return out_v, out_i


def _extract_top_paired(vals, ids, n, big):
    half = vals.shape[0] // 2
    first = vals[:half] >= vals[half:]
    win_v = jnp.where(first, vals[:half], vals[half:])
    win_i = jnp.where(first, ids[:half], ids[half:])
    los_v = jnp.where(first, vals[half:], vals[:half])
    los_i = jnp.where(first, ids[half:], ids[:half])
    out_v, out_i = [], []
    for _ in range(n):
        m = jnp.max(win_v, axis=0, keepdims=True)
        pick = jnp.min(jnp.where(win_v == m, win_i, big), axis=0, keepdims=True)
        out_v.append(m)
        out_i.append(pick)
        hit = win_i == pick
        win_v = jnp.where(hit, los_v, win_v)
        win_i = jnp.where(hit, los_i, win_i)
        los_v = jnp.where(hit, -jnp.inf, los_v)
    return out_v, out_i


def _stack_rows(rows):
    n = len(rows)
    rid = lax.broadcasted_iota(jnp.int32, (n, rows[0].shape[1]), 0)
    out = jnp.broadcast_to(rows[0], rid.shape)
    for k in range(1, n):
        out = jnp.where(rid == k, rows[k], out)
    return out


def _peer_route_kernel(x_ref, nw_ref, wq_ref, keys_ref, xn_ref, idx_ref, g_ref, qt_ref, it_ref, gt_ref):
    tm = x_ref.shape[0]
    x = x_ref[...]
    ms = jnp.mean(x * x, axis=-1, keepdims=True)
    xn = ((x * lax.rsqrt(ms + RMS_EPS)) * nw_ref[...]).astype(BF16)
    xn_ref[...] = xn
    qt_ref[...] = lax.dot_general(wq_ref[...], xn, (((1,), (1,)), ((), ())),
                                  preferred_element_type=F32).astype(BF16)

    key_id = lax.broadcasted_iota(jnp.int32, (PEER_N_KEYS, ROUTE_LANES), 0).astype(F32)
    r80 = lax.broadcasted_iota(jnp.int32, (N_CAND, ROUTE_LANES), 0)
    rank_a = jnp.where(r80 < 16, 0, jnp.where(r80 < 72, 1 + lax.shift_right_logical(r80 - 16, 3), r80 - 64))
    rank_b = jnp.where(r80 < 16, r80, jnp.where(r80 < 72, (r80 - 16) & 7, 0))
    code = (rank_a * PEER_TOPK + rank_b).astype(F32)

    def route(h, lane0):
        lanes = slice(lane0, lane0 + ROUTE_LANES)
        sv, si = [], []
        for s in range(2):
            qhs = qt_ref[pl.ds(pl.multiple_of((2 * h + s) * PEER_HALF, PEER_HALF), PEER_HALF), lanes]
            sc = jnp.dot(keys_ref[s], qhs, preferred_element_type=F32)
            v, i = _extract_top_paired(sc, key_id, PEER_TOPK, float(PEER_N_KEYS))
            sv.append(v)
            si.append(i)
        sv1 = _stack_rows(sv[1])
        si1 = _stack_rows(si[1])
        cand = [sv[0][0] + sv1]
        eid = [si[0][0] * PEER_N_KEYS + si1]
        for a in range(1, 8):
            cand.append(sv[0][a] + sv1[0:8])
            eid.append(si[0][a] * PEER_N_KEYS + si1[0:8])
        cand.append(_stack_rows(sv[0][8:16]) + sv1[0:1])
        eid.append(_stack_rows(si[0][8:16]) * PEER_N_KEYS + si1[0:1])
        cand = jnp.concatenate(cand, axis=0)
        eid = jnp.concatenate(eid, axis=0)
        cv, cc = _extract_top(cand, code * PEER_EXPERTS + eid, PEER_TOPK, float(256 * PEER_EXPERTS))
        picked = [c_k - jnp.floor(c_k * (1.0 / PEER_EXPERTS)) * PEER_EXPERTS for c_k in cc]
        cvs = _stack_rows(cv)
        ex = jnp.exp(cvs - cvs[0:1])
        gate = ex / jnp.sum(ex, axis=0, keepdims=True)
        rows = pl.ds(pl.multiple_of(h * PEER_TOPK, PEER_TOPK), PEER_TOPK)
        it_ref[rows, lanes] = _stack_rows(picked)
        gt_ref[rows, lanes] = gate

    def body(h, carry):
        for lane0 in range(0, tm, ROUTE_LANES):
            route(h, lane0)
        return carry

    lax.fori_loop(0, PEER_HEADS, body, 0, unroll=4)
    idx_ref[...] = it_ref[...].T.astype(jnp.int32)
    g_ref[...] = gt_ref[...].T


def peer_route(x2d, norm_w, wq_t, keys, tm=256):
    T = x2d.shape[0]
    npick = PEER_HEADS * PEER_TOPK
    return pl.pallas_call(
        _peer_route_kernel,
        grid=(T // tm,),
        in_specs=[
            pl.BlockSpec((tm, D_MODEL), lambda i: (i, 0)),
            pl.BlockSpec((1, D_MODEL), lambda i: (0, 0)),
            pl.BlockSpec((2 * PEER_HEADS * PEER_HALF, D_MODEL), lambda i: (0, 0)),
            pl.BlockSpec((2, PEER_N_KEYS, PEER_HALF), lambda i: (0, 0, 0)),
        ],
        out_specs=[
            pl.BlockSpec((tm, D_MODEL), lambda i: (i, 0)),
            pl.BlockSpec((tm, npick), lambda i: (i, 0)),
            pl.BlockSpec((tm, npick), lambda i: (i, 0)),
        ],
        out_shape=[
            jax.ShapeDtypeStruct((T, D_MODEL), BF16),
            jax.ShapeDtypeStruct((T, npick), jnp.int32),
            jax.ShapeDtypeStruct((T, npick), F32),
        ],
        scratch_shapes=[
            pltpu.VMEM((2 * PEER_HEADS * PEER_HALF, tm), BF16),
            pltpu.VMEM((npick, tm), F32),
            pltpu.VMEM((npick, tm), F32),
        ],
        compiler_params=_cparams(1),
        name="peer_route",
    )(x2d, norm_w, wq_t, keys)


GATE_PITCH = 136
GATE_UNROLL = 64


def _peer_gates_kernel(idx_ref, g_ref, gd_ref, g3_ref):
    tm = idx_ref.shape[0]
    rid = lax.broadcasted_iota(jnp.int32, (PEER_N_KEYS, 128), 0)

    def body(tb, carry):
        t0 = pl.multiple_of(tb * GATE_UNROLL, GATE_UNROLL)
        e = idx_ref[pl.ds(t0, GATE_UNROLL), :]
        g = g_ref[pl.ds(t0, GATE_UNROLL), :]
        i1 = lax.shift_right_logical(e, 7)
        i2 = e & (PEER_N_KEYS - 1)
        for u in range(GATE_UNROLL):
            a_t = jnp.where(rid == i1[u:u + 1], g[u:u + 1], 0.0).astype(BF16)
            b_t = jnp.where(rid == i2[u:u + 1], 1.0, 0.0).astype(BF16)
            gd = lax.dot_general(a_t, b_t, (((1,), (1,)), ((), ())), preferred_element_type=F32)
            g3_ref[pl.ds(t0 + u, PEER_N_KEYS, stride=GATE_PITCH), :] = gd
        return carry

    lax.fori_loop(0, tm // GATE_UNROLL, body, 0)
    for a in range(PEER_N_KEYS):
        gd_ref[:, a * PEER_N_KEYS:(a + 1) * PEER_N_KEYS] = g3_ref[a * GATE_PITCH:a * GATE_PITCH + tm, :].astype(BF16)


def peer_gates(idx, g, tm=128):
    T, npick = idx.shape
    return pl.pallas_call(
        _peer_gates_kernel,
        grid=(T // tm,),
        in_specs=[pl.BlockSpec((tm, npick), lambda i: (i, 0)), pl.BlockSpec((tm, npick), lambda i: (i, 0))],
        out_specs=pl.BlockSpec((tm, PEER_EXPERTS), lambda i: (i, 0)),
        out_shape=jax.ShapeDtypeStruct((T, PEER_EXPERTS), BF16),
        scratch_shapes=[pltpu.VMEM((PEER_N_KEYS * GATE_PITCH, PEER_N_KEYS), F32)],
        compiler_params=_cparams(1),
        name="peer_gates",
    )(idx, g)


def _peer_experts_kernel(x_ref, xn_ref, gd_ref, ut_ref, v_ref, fw_ref, out_ref, acc_ref, *, final_norm):
    j = pl.program_id(1)

    @pl.when(j == 0)
    def _():
        acc_ref[...] = jnp.zeros_like(acc_ref)

    z = jnp.dot(xn_ref[...], ut_ref[...], preferred_element_type=F32)
    hid = 0.5 * z * (1.0 + lax.erf(z * np.float32(np.sqrt(0.5))))
    w = (gd_ref[...].astype(F32) * hid).astype(BF16)
    acc_ref[...] += jnp.dot(w, v_ref[...], preferred_element_type=F32)

    @pl.when(j == pl.num_programs(1) - 1)
    def _():
        y = x_ref[...] + acc_ref[...]
        if final_norm:
            ms = jnp.mean(y * y, axis=-1, keepdims=True)
            y = (y * lax.rsqrt(ms + RMS_EPS)) * fw_ref[...]
        out_ref[...] = y


def peer_experts(x2d, xn, gd, u_t, v, final_w, final_norm, tm=512, te=2048):
    T = x2d.shape[0]
    return pl.pallas_call(
        functools.partial(_peer_experts_kernel, final_norm=final_norm),
        grid=(T // tm, PEER_EXPERTS // te),
        in_specs=[
            pl.BlockSpec((tm, D_MODEL), lambda i, j: (i, 0)),
            pl.BlockSpec((tm, D_MODEL), lambda i, j: (i, 0)),
            pl.BlockSpec((tm, te), lambda i, j: (i, j)),
            pl.BlockSpec((D_MODEL, te), lambda i, j: (0, j)),
            pl.BlockSpec((te, D_MODEL), lambda i, j: (j, 0)),
            pl.BlockSpec((1, D_MODEL), lambda i, j: (0, 0)),
        ],
        out_specs=pl.BlockSpec((tm, D_MODEL), lambda i, j: (i, 0)),
        out_shape=jax.ShapeDtypeStruct((T, D_MODEL), F32),
        scratch_shapes=[pltpu.VMEM((tm, D_MODEL), F32)],
        compiler_params=_cparams(2),
        name="peer_experts_final" if final_norm else "peer_experts",
    )(x2d, xn, gd, u_t, v, final_w)


def _t5_bucket(rel):
    half = REL_BUCKETS // 2
    max_exact = half // 2
    n = np.abs(rel)
    large = max_exact + (np.log(np.maximum(n, 1) / max_exact) / np.log(REL_MAX_DISTANCE / max_exact)
                         * (half - max_exact)).astype(np.int32)
    large = np.minimum(large, half - 1)
    return np.where(rel > 0, half, 0) + np.where(n < max_exact, n, large)


def prep_attn_bias(rel_bias, gi, dil):
    i = np.arange(ATT_TQ)[:, None]
    j = np.arange(ATT_TK)[None, :]
    rel = j - ATT_SPAN - i
    bucket = _t5_bucket(rel * dil)
    table = rel_bias[bucket][:, :, gi * HEADS_PER_GROUP:(gi + 1) * HEADS_PER_GROUP]
    table = jnp.where((np.abs(rel) <= ATT_SPAN)[:, :, None], table.astype(F32), NEG)
    return jnp.transpose(table, (2, 0, 1))


def _per_head_lanes(t):
    lead = t.shape[:-1]
    t = t.reshape(lead + (2, SSD_GROUPS, 4))
    t = jnp.swapaxes(t, -3, -2)
    t = jnp.broadcast_to(t[..., None], lead + (SSD_GROUPS, 2, 4, DT_REP))
    return t.reshape(lead + (SSD_GROUPS * 128,))


def prep_w_in(w_in_l):
    w_main = jnp.concatenate(
        [w_in_l[:, :OFF_XBC], w_in_l[:, OFF_GATE:], w_in_l[:, OFF_XBC:OFF_DT], w_in_l[:, OFF_Q:OFF_GATE]], axis=1
    ).astype(BF16)
    w_dtx = _per_head_lanes(w_in_l[:, OFF_DT:OFF_Q].astype(BF16))
    return w_main, w_dtx


def prep_ssd_params(dt_bias_l, a_log_l, d_skip_l):
    dtb_g = _per_head_lanes(dt_bias_l.reshape(-1)).reshape(SSD_GROUPS, 1, 128)
    alog_g = _per_head_lanes(a_log_l.reshape(-1)).reshape(SSD_GROUPS, 1, 128)
    dsk_g = jnp.repeat(d_skip_l, SSD_HEAD_DIM).reshape(SSD_GROUPS, 1, 256)
    return dtb_g, alog_g, dsk_g


def trunk_layer(x2d, B, L, biases, norm1_w, w_in, conv_w, conv_b, a_log, dt_bias, d_skip, ssd_norm_w,
                w_ssd_out, w_attn_out, w_out, norm2_w, w_query, sub_keys, expert_u, expert_v, final_w, final_norm):
    w_main, w_dtx = prep_w_in(w_in)
    proj, dtx = in_proj(x2d, norm1_w.reshape(1, -1), w_main, w_dtx)
    proj3 = proj.reshape(B, L, PROJ_COLS)
    yssd = ssd_mixer(proj3, dtx.reshape(B, L, DTX_COLS), conv_w, conv_b.reshape(1, -1),
                     *prep_ssd_params(dt_bias, a_log, d_skip))
    outs, lses = [], []
    for gi, (_, dil) in enumerate(DILATION_GROUPS):
        o, lse = dilated_attention(proj3, biases[gi], gi, dil)
        outs.append(o.reshape(B * L, ATTN_OUT_WIDTH))
        lses.append(lse.reshape(B * L, 128))
    x1 = mix_project(x2d, yssd.reshape(B * L, D_INNER), proj, outs, lses, ssd_norm_w.reshape(1, -1),
                     w_ssd_out.astype(BF16), w_attn_out.astype(BF16), w_out.astype(BF16))
    xn, idx, gate = peer_route(x1, norm2_w.reshape(1, -1), w_query.T.astype(BF16), sub_keys.astype(BF16))
    gd = peer_gates(idx, gate)
    return peer_experts(x1, xn, gd, expert_u.T.astype(BF16), expert_v.astype(BF16), final_w.reshape(1, -1), final_norm)


def kernel(x_prompt, x_sample, rel_bias, norm1_w, w_in, conv_w, conv_b, A_log, dt_bias, D_skip, ssd_norm_w,
           w_ssd_out, w_attn_out, w_out, norm2_w, w_query, sub_keys, expert_u, expert_v, final_norm_w):
    nb_prompt = x_prompt.shape[0]
    x = jnp.concatenate([x_prompt, x_sample], axis=0)
    B, L, _ = x.shape
    x2d = x.reshape(B * L, D_MODEL)
    biases = [prep_attn_bias(rel_bias, gi, dil) for gi, (_, dil) in enumerate(DILATION_GROUPS)]
    for l in range(DEPTH):
        x2d = trunk_layer(x2d, B, L, biases, norm1_w[l], w_in[l], conv_w[l], conv_b[l], A_log[l], dt_bias[l],
                          D_skip[l], ssd_norm_w[l], w_ssd_out[l], w_attn_out[l], w_out[l], norm2_w[l], w_query[l],
                          sub_keys[l], expert_u[l], expert_v[l], final_norm_w, l == DEPTH - 1)
    y = x2d.reshape(B, L, D_MODEL)
    return y[:nb_prompt], y[nb_prompt:]
```

```python
import functools

import numpy as np
import jax
import jax.numpy as jnp
from jax import lax
from jax.experimental import pallas as pl
from jax.experimental.pallas import tpu as pltpu

F32 = jnp.float32
BF16 = jnp.bfloat16
HIGHEST = lax.Precision.HIGHEST

D_MODEL = 1024
DEPTH = 2
D_INNER = 2048
SSD_HEADS = 32
SSD_HEAD_DIM = 64
SSD_GROUPS = 8
SSD_STATE = 128
CHUNK = 128
D_CONV = 5
GN = SSD_GROUPS * SSD_STATE
CONV_DIM = D_INNER + 2 * GN
ATTN_HEAD_DIM = 64
HEADS_PER_GROUP = 8
DILATION_GROUPS = ((128, 1), (512, 4), (2048, 16))
ATTN_WIDTH = 1536
ATTN_OUT_WIDTH = 512
REL_BUCKETS = 32
REL_MAX_DISTANCE = 1024
PEER_HEADS = 8
PEER_N_KEYS = 128
PEER_EXPERTS = PEER_N_KEYS * PEER_N_KEYS
PEER_HALF = 128
PEER_TOPK = 16
RMS_EPS = 1e-6

OFF_XBC = D_INNER
OFF_DT = OFF_XBC + CONV_DIM
OFF_Q = OFF_DT + 2 * SSD_HEADS

OFF_K = OFF_Q + ATTN_WIDTH
OFF_V = OFF_K + ATTN_WIDTH
OFF_GATE = OFF_V + ATTN_WIDTH

PC_Z = 0
PC_GATE = D_INNER
PC_XBC = PC_GATE + 2 * D_MODEL
PC_Q = PC_XBC + CONV_DIM
PC_K = PC_Q + ATTN_WIDTH
PC_V = PC_K + ATTN_WIDTH
PROJ_COLS = PC_V + ATTN_WIDTH
DT_REP = 16
DTX_COLS = SSD_GROUPS * 128

ATT_SPAN = 64
ATT_TQ = 128
ATT_TK = 256
NEG = -1e30

VMEM_LIMIT = 56 * 1024 * 1024


def _cparams(n_axes):
    return pltpu.CompilerParams(
        dimension_semantics=("arbitrary",) * n_axes, vmem_limit_bytes=VMEM_LIMIT
    )


def _inproj_kernel(x_ref, nw_ref, w_ref, wdt_ref, o_ref, dt_ref, xn_ref):
    @pl.when(pl.program_id(1) == 0)
    def _():
        x = x_ref[...]
        ms = jnp.mean(x * x, axis=-1, keepdims=True)
        xn = ((x * lax.rsqrt(ms + RMS_EPS)) * nw_ref[...]).astype(BF16)
        xn_ref[...] = xn
        dt_ref[...] = jnp.dot(xn, wdt_ref[...], preferred_element_type=F32)

    o_ref[...] = jnp.dot(xn_ref[...], w_ref[...], preferred_element_type=F32).astype(BF16)


def in_proj(x2d, norm_w, w_main, w_dtx, tm=1024, tn=2560):
    T = x2d.shape[0]
    nc = w_main.shape[1]
    return pl.pallas_call(
        _inproj_kernel,
        grid=(T // tm, nc // tn),
        in_specs=[
            pl.BlockSpec((tm, D_MODEL), lambda i, j: (i, 0)),
            pl.BlockSpec((1, D_MODEL), lambda i, j: (0, 0)),
            pl.BlockSpec((D_MODEL, tn), lambda i, j: (0, j)),
            pl.BlockSpec((D_MODEL, DTX_COLS), lambda i, j: (0, 0)),
        ],
        out_specs=[
            pl.BlockSpec((tm, tn), lambda i, j: (i, j)),
            pl.BlockSpec((tm, DTX_COLS), lambda i, j: (i, 0)),
        ],
        out_shape=[
            jax.ShapeDtypeStruct((T, nc), BF16),
            jax.ShapeDtypeStruct((T, DTX_COLS), F32),
        ],
        scratch_shapes=[pltpu.VMEM((tm, D_MODEL), BF16)],
        compiler_params=_cparams(2),
        name="in_proj",
    )(x2d, norm_w, w_main, w_dtx)


SSD_HALO = 64


def _split3(x):
    hi = x.astype(BF16).astype(F32)
    r1 = x - hi
    mid = r1.astype(BF16).astype(F32)
    lo = (r1 - mid).astype(BF16).astype(F32)
    return hi, mid, lo


def _ssd_kernel(x_ref, b_ref, c_ref, dt_ref, cwx_ref, cwb_ref, cwc_ref, cbx_ref, cbb_ref, cbc_ref,
                dtb_ref, alog_ref, dsk_ref, y_ref,
                pad_ref, xs_ref, bt_ref, cm_ref, yacc_ref, st_ref, pk_ref, ng_ref):
    L = x_ref.shape[1]
    nch = L // CHUNK
    cw = jnp.concatenate([cwx_ref[...], cwb_ref[...], cwc_ref[...]], axis=1)
    cb = jnp.concatenate([cbx_ref[...], cbb_ref[...], cbc_ref[...]], axis=1)
    dsk = dsk_ref[0]

    zhalo = jnp.zeros((SSD_HALO, 512), BF16)
    pad_ref[0:SSD_HALO, :] = zhalo
    pad_ref[L + SSD_HALO:L + 2 * SSD_HALO, :] = zhalo

    def copy_body(c, carry):
        r = pl.multiple_of(c * CHUNK, CHUNK)
        dst = pl.ds(r + SSD_HALO, CHUNK)
        pad_ref[dst, 0:256] = x_ref[0, pl.ds(r, CHUNK), :]
        pad_ref[dst, 256:384] = b_ref[0, pl.ds(r, CHUNK), :]
        pad_ref[dst, 384:512] = c_ref[0, pl.ds(r, CHUNK), :]
        return carry

    lax.fori_loop(0, nch, copy_body, 0)

    taps = (-2, -1, 1, 2)
    srow = lax.broadcasted_iota(jnp.int32, (len(taps) * CHUNK, CHUNK + 2 * SSD_HALO), 0)
    scol = lax.broadcasted_iota(jnp.int32, (len(taps) * CHUNK, CHUNK + 2 * SSD_HALO), 1)
    soff = jnp.where(srow < CHUNK, taps[0], jnp.where(srow < 2 * CHUNK, taps[1], jnp.where(srow < 3 * CHUNK, taps[2], taps[3])))
    shift_mat = (scol == (srow & (CHUNK - 1)) + SSD_HALO + soff).astype(BF16)

    def conv_body(c, carry):
        r = pl.multiple_of(c * CHUNK, CHUNK)
        win = pad_ref[pl.ds(r, CHUNK + 2 * SSD_HALO), :]
        sh = jnp.dot(shift_mat, win, preferred_element_type=F32)
        acc = win[SSD_HALO:SSD_HALO + CHUNK, :].astype(F32) * cw[2:3, :] + cb
        for i, w in enumerate((0, 1, 3, 4)):
            acc = acc + sh[i * CHUNK:(i + 1) * CHUNK, :] * cw[w:w + 1, :]
        act = acc * jax.nn.sigmoid(acc)
        xs = act[:, 0:256]
        xs_ref[pl.ds(r, CHUNK), :] = xs
        yacc_ref[pl.ds(r, CHUNK), :] = xs * dsk
        bt_ref[c] = act[:, 256:384].T.astype(BF16)
        cm_ref[pl.ds(r, CHUNK), :] = act[:, 384:512].astype(BF16)
        return carry

    lax.fori_loop(0, nch, conv_body, 0, unroll=2)

    st_ref[...] = jnp.zeros_like(st_ref)

    row = lax.broadcasted_iota(jnp.int32, (CHUNK, CHUNK), 0)
    col = lax.broadcasted_iota(jnp.int32, (CHUNK, CHUNK), 1)
    lane_lt64 = col < 64
    m16 = col & (DT_REP - 1)
    a_neg = -jnp.exp(alog_ref[0])
    dtb = dtb_ref[0]
    erow = lax.broadcasted_iota(jnp.int32, (CHUNK, 2 * CHUNK), 0)
    ecol = lax.broadcasted_iota(jnp.int32, (CHUNK, 2 * CHUNK), 1)

    def selectors(d):
        rel = erow - (64 * d + DT_REP * lax.shift_right_logical(ecol, 6))
        sel_cs = ((rel >= 0) & (rel < 3)).astype(BF16)
        sel_dt = ((rel >= 6) & (rel < 9)).astype(BF16)
        return sel_cs, sel_dt

    sels = (selectors(0), selectors(1))
    tri3 = tuple(jnp.concatenate([m.astype(BF16)] * 3, axis=1) for m in (row >= col, row <= col))

    head_rows = [((row >= DT_REP * k) & (row < DT_REP * k + 6)).astype(BF16) for k in range(8)]
    half_lo = lane_lt64.astype(BF16)
    half_hi = 1 - half_lo

    def prep_body(c, carry):
        rows = pl.ds(pl.multiple_of(c * CHUNK, CHUNK), CHUNK)
        dt = jax.nn.softplus(dt_ref[0, rows, :] + dtb)
        a = dt * a_neg
        a3 = jnp.concatenate([t.astype(BF16) for t in _split3(a)], axis=0)
        cs = jnp.where(lane_lt64,
                       jnp.dot(tri3[0], a3, preferred_element_type=F32),
                       jnp.dot(tri3[1], a3, preferred_element_type=F32))
        c_hi, c_mid, c_lo = _split3(cs)
        d_hi, d_mid, d_lo = _split3(dt)
        packed = jnp.where(m16 == 0, c_hi, jnp.where(m16 == 1, c_mid, jnp.where(m16 == 2, c_lo, jnp.where(
            m16 < 6, 1.0, jnp.where(m16 == 6, d_hi, jnp.where(m16 == 7, d_mid, jnp.where(m16 == 8, d_lo, 0.0)))))))
        pk_ref[c] = packed.astype(BF16)
        negs = jnp.where(m16 < 3, 1.0, jnp.where(m16 == 3, -c_hi, jnp.where(m16 == 4, -c_mid, jnp.where(
            m16 == 5, -c_lo, 0.0))))
        negs_t = negs.T.astype(BF16)
        for d in range(2):
            ng_ref[c, d] = jnp.concatenate([negs_t * head_rows[4 * d + h] for h in range(4)], axis=1)
        return carry

    lax.fori_loop(0, nch, prep_body, 0, unroll=2)

    def one_pass(d, c):
        r = pl.multiple_of(c * CHUNK, CHUNK)
        rows = pl.ds(r, CHUNK)
        mask = (row >= col) if d == 0 else (row <= col)
        last = CHUNK - 1 if d == 0 else 0
        packed = pk_ref[c]
        sel_cs, sel_dt = sels[d]
        cs2 = jnp.dot(packed, sel_cs, preferred_element_type=F32)
        dt2 = jnp.dot(packed, sel_dt, preferred_element_type=F32)
        diff = jnp.dot(packed, ng_ref[c, d], preferred_element_type=F32)
        tot2 = cs2[last:last + 1, :]
        ein2 = jnp.exp(cs2)
        dd2 = jnp.exp(tot2 - cs2) * dt2
        cd2 = jnp.exp(tot2)
        cmat = cm_ref[rows, :]
        btc = bt_ref[c]
        cbm = jnp.dot(cmat, btc, preferred_element_type=F32)
        xs = xs_ref[rows, :]
        xg = (xs * dt2).astype(BF16)
        xdec = (xs * dd2).astype(BF16)
        ydiag = []
        for p in range(2):
            cols = slice(128 * p, 128 * p + 128)
            gs = []
            for hh in range(2):
                h = 2 * p + hh
                seg = jnp.where(mask, jnp.exp(diff[:, 128 * h:128 * h + 128]), 0.0)
                gs.append((cbm * seg).astype(BF16))
            x2 = xg[:, cols]
            ydiag.append(jnp.dot(jnp.concatenate(gs, axis=1), jnp.concatenate([x2 * half_lo, x2 * half_hi], axis=0),
                                 preferred_element_type=F32))
        st = st_ref[d]
        yoff = jnp.dot(cmat, st.astype(BF16), preferred_element_type=F32) * ein2
        yacc_ref[rows, :] += jnp.concatenate(ydiag, axis=1) + yoff
        st_ref[d] = st * cd2 + jnp.dot(btc, xdec, preferred_element_type=F32)

    def main_body(i, carry):
        one_pass(0, 2 * i)
        one_pass(1, nch - 1 - 2 * i)
        one_pass(0, 2 * i + 1)
        one_pass(1, nch - 2 - 2 * i)
        return carry

    lax.fori_loop(0, nch // 2, main_body, 0)

    def out_body(c, carry):
        rows = pl.ds(pl.multiple_of(c * CHUNK, CHUNK), CHUNK)
        y_ref[0, rows, :] = yacc_ref[rows, :].astype(BF16)
        return carry

    lax.fori_loop(0, nch, out_body, 0)


def ssd_mixer(proj3, dtx3, conv_w, conv_b, dtb_g, alog_g, dsk_g):
    B, L, _ = proj3.shape
    xb = PC_XBC // 256
    bb = (PC_XBC + D_INNER) // 128
    cbk = (PC_XBC + D_INNER + GN) // 128
    return pl.pallas_call(
        _ssd_kernel,
        grid=(B, SSD_GROUPS),
        in_specs=[
            pl.BlockSpec((1, L, 256), lambda b, g: (b, 0, xb + g)),
            pl.BlockSpec((1, L, 128), lambda b, g: (b, 0, bb + g)),
            pl.BlockSpec((1, L, 128), lambda b, g: (b, 0, cbk + g)),
            pl.BlockSpec((1, L, 128), lambda b, g: (b, 0, g)),
            pl.BlockSpec((D_CONV, 256), lambda b, g: (0, g)),
            pl.BlockSpec((D_CONV, 128), lambda b, g: (0, D_INNER // 128 + g)),
            pl.BlockSpec((D_CONV, 128), lambda b, g: (0, (D_INNER + GN) // 128 + g)),
            pl.BlockSpec((1, 256), lambda b, g: (0, g)),
            pl.BlockSpec((1, 128), lambda b, g: (0, D_INNER // 128 + g)),
            pl.BlockSpec((1, 128), lambda b, g: (0, (D_INNER + GN) // 128 + g)),
            pl.BlockSpec((1, 1, 128), lambda b, g: (g, 0, 0)),
            pl.BlockSpec((1, 1, 128), lambda b, g: (g, 0, 0)),
            pl.BlockSpec((1, 1, 256), lambda b, g: (g, 0, 0)),
        ],
        out_specs=pl.BlockSpec((1, L, 256), lambda b, g: (b, 0, g)),
        out_shape=jax.ShapeDtypeStruct((B, L, D_INNER), BF16),
        scratch_shapes=[
            pltpu.VMEM((L + 2 * SSD_HALO, 512), BF16),
            pltpu.VMEM((L, 256), F32),
            pltpu.VMEM((L // CHUNK, SSD_STATE, CHUNK), BF16),
            pltpu.VMEM((L, SSD_STATE), BF16),
            pltpu.VMEM((L, 256), F32),
            pltpu.VMEM((2, SSD_STATE, 256), F32),
            pltpu.VMEM((L // CHUNK, CHUNK, 128), BF16),
            pltpu.VMEM((L // CHUNK, 2, 128, 4 * CHUNK), BF16),
        ],
        compiler_params=_cparams(2),
        name="ssd_mixer",
    )(proj3, proj3, proj3, dtx3, conv_w, conv_w, conv_w, conv_b, conv_b, conv_b, dtb_g, alog_g, dsk_g)


def _attn_kernel(q_ref, k_ref, v_ref, bias_ref, o_ref, lse_ref, kp_ref, vp_ref, *stage, dil):
    L = q_ref.shape[1]
    M = L // dil
    nt = M // ATT_TQ
    width = q_ref.shape[2]
    nslab = width // 128
    zpad = jnp.zeros((ATT_SPAN, width), BF16)
    kp_ref[0:ATT_SPAN, :] = zpad
    kp_ref[M + ATT_SPAN:M + 2 * ATT_SPAN, :] = zpad
    vp_ref[0:ATT_SPAN, :] = zpad
    vp_ref[M + ATT_SPAN:M + 2 * ATT_SPAN, :] = zpad

    if dil > 1:
        qf_ref, kf_ref, vf_ref, of_ref, lf_ref = stage

        def stage_body(c, carry):
            rows = pl.ds(pl.multiple_of(c * ATT_TQ, ATT_TQ), ATT_TQ)
            for s in range(nslab):
                cols = slice(128 * s, 128 * s + 128)
                qf_ref[s, rows, :] = q_ref[0, rows, cols].astype(F32)
                kf_ref[s, rows, :] = k_ref[0, rows, cols].astype(F32)
                vf_ref[s, rows, :] = v_ref[0, rows, cols].astype(F32)
            return carry

        lax.fori_loop(0, L // ATT_TQ, stage_body, 0)
    else:
        qf_ref = kf_ref = vf_ref = of_ref = lf_ref = None

    def sub_rows(r, t):
        return pl.ds(r + t * (ATT_TQ * dil), ATT_TQ, stride=dil)

    def load_tile(ref, ref_f32, r, t):
        if dil == 1:
            return ref[0, pl.ds(pl.multiple_of(t * ATT_TQ, ATT_TQ), ATT_TQ), :]
        return jnp.concatenate([ref_f32[s, sub_rows(r, t), :] for s in range(nslab)], axis=1).astype(BF16)

    lane = lax.broadcasted_iota(jnp.int32, (ATT_TQ, 128), 1)
    lane_lt64 = lane < 64

    def residue_body(r, carry):
        def copy_body(t, c2):
            dst = pl.ds(pl.multiple_of(t * ATT_TQ, ATT_TQ) + ATT_SPAN, ATT_TQ)
            kp_ref[dst, :] = load_tile(k_ref, kf_ref, r, t)
            vp_ref[dst, :] = load_tile(v_ref, vf_ref, r, t)
            return c2

        lax.fori_loop(0, nt, copy_body, 0)

        def tile_body(t, c2):
            r0 = pl.multiple_of(t * ATT_TQ, ATT_TQ)
            q = load_tile(q_ref, qf_ref, r, t) * jnp.asarray(ATTN_HEAD_DIM ** -0.5, BF16)
            kw = kp_ref[pl.ds(r0, ATT_TK), :]
            vw = vp_ref[pl.ds(r0, ATT_TK), :]
            edge = jnp.where(t == 0, 1, 0) + jnp.where(t == nt - 1, 2, 0)
            lse_tile = jnp.zeros((ATT_TQ, 128), F32)
            outs = []
            for p in range(HEADS_PER_GROUP // 2):
                cols = slice(128 * p, 128 * p + 128)
                q2, k2, v2 = q[:, cols], kw[:, cols], vw[:, cols]
                pv, dens = [], []
                for hh in range(2):
                    h = 2 * p + hh
                    qm = jnp.where(lane_lt64 if hh == 0 else jnp.logical_not(lane_lt64), q2, jnp.zeros_like(q2))
                    s = lax.dot_general(qm, k2, (((1,), (1,)), ((), ())), preferred_element_type=F32)
                    s = s + bias_ref[edge, h]
                    m = jnp.max(s, axis=-1, keepdims=True)
                    pexp = jnp.exp(s - m)
                    den = jnp.sum(pexp, axis=-1, keepdims=True)
                    pv.append(jnp.dot(pexp.astype(BF16), v2, preferred_element_type=F32))
                    dens.append(den)
                    lse_tile = jnp.where(lane == h, m + jnp.log(den), lse_tile)
                outs.append(jnp.where(lane_lt64, pv[0], pv[1]) / jnp.where(lane_lt64, dens[0], dens[1]))
            if dil == 1:
                o_ref[0, pl.ds(r0, ATT_TQ), :] = jnp.concatenate(outs, axis=1).astype(BF16)
                lse_ref[0, pl.ds(r0, ATT_TQ), :] = lse_tile
            else:
                for p, o2 in enumerate(outs):
                    of_ref[p, sub_rows(r, t), :] = o2
                lf_ref[sub_rows(r, t), :] = lse_tile
            return c2

        lax.fori_loop(0, nt, tile_body, 0, unroll=2 if nt > 1 else 1)
        return carry

    if dil == 1:
        residue_body(0, 0)
    else:
        lax.fori_loop(0, dil, residue_body, 0)

        def out_body(c, carry):
            rows = pl.ds(pl.multiple_of(c * ATT_TQ, ATT_TQ), ATT_TQ)
            o_ref[0, rows, :] = jnp.concatenate([of_ref[s, rows, :] for s in range(nslab)], axis=1).astype(BF16)
            lse_ref[0, rows, :] = lf_ref[rows, :]
            return carry

        lax.fori_loop(0, L // ATT_TQ, out_body, 0)


def dilated_attention(proj3, bias_g, gi, dil):
    B, L, _ = proj3.shape
    M = L // dil
    width = HEADS_PER_GROUP * ATTN_HEAD_DIM
    nslab = width // 128
    qb, kb, vb = PC_Q // width + gi, PC_K // width + gi, PC_V // width + gi
    scratch = [
        pltpu.VMEM((M + 2 * ATT_SPAN, width), BF16),
        pltpu.VMEM((M + 2 * ATT_SPAN, width), BF16),
    ]
    if dil > 1:
        scratch += [pltpu.VMEM((nslab, L, 128), F32)] * 4 + [pltpu.VMEM((L, 128), F32)]
    return pl.pallas_call(
        functools.partial(_attn_kernel, dil=dil),
        grid=(B,),
        in_specs=[
            pl.BlockSpec((1, L, width), lambda b: (b, 0, qb)),
            pl.BlockSpec((1, L, width), lambda b: (b, 0, kb)),
            pl.BlockSpec((1, L, width), lambda b: (b, 0, vb)),
            pl.BlockSpec((4, HEADS_PER_GROUP, ATT_TQ, ATT_TK), lambda b: (0, 0, 0, 0)),
        ],
        out_specs=[
            pl.BlockSpec((1, L, width), lambda b: (b, 0, 0)),
            pl.BlockSpec((1, L, 128), lambda b: (b, 0, 0)),
        ],
        out_shape=[
            jax.ShapeDtypeStruct((B, L, width), BF16),
            jax.ShapeDtypeStruct((B, L, 128), F32),
        ],
        scratch_shapes=scratch,
        compiler_params=_cparams(1),
        name=f"dilated_attention_d{dil}",
    )(proj3, proj3, proj3, bias_g)


def _mix_kernel(x_ref, y_ref, z_ref, g_ref, o0_ref, o1_ref, o2_ref, l0_ref, l1_ref, l2_ref,
                nw_ref, wssd_ref, watt_ref, wout_ref, out_ref):
    z = z_ref[...].astype(F32)
    yz = y_ref[...].astype(F32) * (z * jax.nn.sigmoid(z))
    ms = jnp.mean(yz * yz, axis=-1, keepdims=True)
    yn = ((yz * lax.rsqrt(ms + RMS_EPS)) * nw_ref[...]).astype(BF16)
    pa = jnp.dot(yn, wssd_ref[...], preferred_element_type=F32)

    ls = [l0_ref[...], l1_ref[...], l2_ref[...]]
    mx = jnp.maximum(jnp.maximum(ls[0], ls[1]), ls[2])
    es = [jnp.exp(l - mx) for l in ls]
    den = es[0] + es[1] + es[2]
    erow = lax.broadcasted_iota(jnp.int32, (128, ATTN_OUT_WIDTH), 0)
    ecol = lax.broadcasted_iota(jnp.int32, (128, ATTN_OUT_WIDTH), 1)
    expand = (erow == ecol // ATTN_HEAD_DIM).astype(F32)
    yatt = None
    for e, o_ref in zip(es, (o0_ref, o1_ref, o2_ref)):
        wexp = jnp.dot(e / den, expand, precision=HIGHEST, preferred_element_type=F32)
        term = wexp * o_ref[...].astype(F32)
        yatt = term if yatt is None else yatt + term
    pb = jnp.dot(yatt.astype(BF16), watt_ref[...], preferred_element_type=F32)

    gates = jax.nn.sigmoid(g_ref[...].astype(F32))
    merged = gates[:, :D_MODEL] * pa + gates[:, D_MODEL:] * pb
    out_ref[...] = x_ref[...] + jnp.dot(merged.astype(BF16), wout_ref[...], preferred_element_type=F32)


def mix_project(x2d, yssd, proj, outs, lses, norm_w, w_ssd, w_att, w_out, tm=512):
    T = x2d.shape[0]
    row = lambda i: (i, 0)
    const = lambda i: (0, 0)
    return pl.pallas_call(
        _mix_kernel,
        grid=(T // tm,),
        in_specs=[
            pl.BlockSpec((tm, D_MODEL), row),
            pl.BlockSpec((tm, D_INNER), row),
            pl.BlockSpec((tm, D_INNER), lambda i: (i, PC_Z // D_INNER)),
            pl.BlockSpec((tm, 2 * D_MODEL), lambda i: (i, PC_GATE // (2 * D_MODEL))),
            pl.BlockSpec((tm, ATTN_OUT_WIDTH), row),
            pl.BlockSpec((tm, ATTN_OUT_WIDTH), row),
            pl.BlockSpec((tm, ATTN_OUT_WIDTH), row),
            pl.BlockSpec((tm, 128), row),
            pl.BlockSpec((tm, 128), row),
            pl.BlockSpec((tm, 128), row),
            pl.BlockSpec((1, D_INNER), const),
            pl.BlockSpec((D_INNER, D_MODEL), const),
            pl.BlockSpec((ATTN_OUT_WIDTH, D_MODEL), const),
            pl.BlockSpec((D_MODEL, D_MODEL), const),
        ],
        out_specs=pl.BlockSpec((tm, D_MODEL), row),
        out_shape=jax.ShapeDtypeStruct((T, D_MODEL), F32),
        compiler_params=_cparams(1),
        name="mix_project",
    )(x2d, yssd, proj, proj, outs[0], outs[1], outs[2], lses[0], lses[1], lses[2], norm_w, w_ssd, w_att, w_out)


ROUTE_LANES = 128
N_CAND = 80


def _extract_top(vals, ids, n, big):
    out_v, out_i = [], []
    for _ in range(n):
        m = jnp.max(vals, axis=0, keepdims=True)
        pick = jnp.min(jnp.where(vals == m, ids, big), axis=0, keepdims=True)
        out_v.append(m)
        out_i.append(pick)
        vals = jnp.where(ids == pick, -jnp.inf, vals)
    return out_v, out_i


def _extract_top_paired(vals, ids, n, big):
    half = vals.shape[0] // 2
    first = vals[:half] >= vals[half:]
    win_v = jnp.where(first, vals[:half], vals[half:])
    win_i = jnp.where(first, ids[:half], ids[half:])
    los_v = jnp.where(first, vals[half:], vals[:half])
    los_i = jnp.where(first, ids[half:], ids[:half])
    out_v, out_i = [], []
    for _ in range(n):
        m = jnp.max(win_v, axis=0, keepdims=True)
        pick = jnp.min(jnp.where(win_v == m, win_i, big), axis=0, keepdims=True)
        out_v.append(m)
        out_i.append(pick)
        hit = win_i == pick
        win_v = jnp.where(hit, los_v, win_v)
        win_i = jnp.where(hit, los_i, win_i)
        los_v = jnp.where(hit, -jnp.inf, los_v)
    return out_v, out_i


def _stack_rows(rows):
    n = len(rows)
    rid = lax.broadcasted_iota(jnp.int32, (n, rows[0].shape[1]), 0)
    out = jnp.broadcast_to(rows[0], rid.shape)
    for k in range(1, n):
        out = jnp.where(rid == k, rows[k], out)
    return out


def _peer_route_kernel(x_ref, nw_ref, wq_ref, keys_ref, xn_ref, idx_ref, g_ref, qt_ref, it_ref, gt_ref):
    tm = x_ref.shape[0]
    x = x_ref[...]
    ms = jnp.mean(x * x, axis=-1, keepdims=True)
    xn = ((x * lax.rsqrt(ms + RMS_EPS)) * nw_ref[...]).astype(BF16)
    xn_ref[...] = xn
    qt_ref[...] = lax.dot_general(wq_ref[...], xn, (((1,), (1,)), ((), ())),
                                  preferred_element_type=F32).astype(BF16)

    key_id = lax.broadcasted_iota(jnp.int32, (PEER_N_KEYS, ROUTE_LANES), 0).astype(F32)
    r80 = lax.broadcasted_iota(jnp.int32, (N_CAND, ROUTE_LANES), 0)
    rank_a = jnp.where(r80 < 16, 0, jnp.where(r80 < 72, 1 + lax.shift_right_logical(r80 - 16, 3), r80 - 64))
    rank_b = jnp.where(r80 < 16, r80, jnp.where(r80 < 72, (r80 - 16) & 7, 0))
    code = (rank_a * PEER_TOPK + rank_b).astype(F32)

    def route(h, lane0):
        lanes = slice(lane0, lane0 + ROUTE_LANES)
        sv, si = [], []
        for s in range(2):
            qhs = qt_ref[pl.ds(pl.multiple_of((2 * h + s) * PEER_HALF, PEER_HALF), PEER_HALF), lanes]
            sc = jnp.dot(keys_ref[s], qhs, preferred_element_type=F32)
            v, i = _extract_top_paired(sc, key_id, PEER_TOPK, float(PEER_N_KEYS))
            sv.append(v)
            si.append(i)
        sv1 = _stack_rows(sv[1])
        si1 = _stack_rows(si[1])
        cand = [sv[0][0] + sv1]
        eid = [si[0][0] * PEER_N_KEYS + si1]
        for a in range(1, 8):
            cand.append(sv[0][a] + sv1[0:8])
            eid.append(si[0][a] * PEER_N_KEYS + si1[0:8])
        cand.append(_stack_rows(sv[0][8:16]) + sv1[0:1])
        eid.append(_stack_rows(si[0][8:16]) * PEER_N_KEYS + si1[0:1])
        cand = jnp.concatenate(cand, axis=0)
        eid = jnp.concatenate(eid, axis=0)
        cv, cc = _extract_top(cand, code * PEER_EXPERTS + eid, PEER_TOPK, float(256 * PEER_EXPERTS))
        picked = [c_k - jnp.floor(c_k * (1.0 / PEER_EXPERTS)) * PEER_EXPERTS for c_k in cc]
        cvs = _stack_rows(cv)
        ex = jnp.exp(cvs - cvs[0:1])
        gate = ex / jnp.sum(ex, axis=0, keepdims=True)
        rows = pl.ds(pl.multiple_of(h * PEER_TOPK, PEER_TOPK), PEER_TOPK)
        it_ref[rows, lanes] = _stack_rows(picked)
        gt_ref[rows, lanes] = gate

    def body(h, carry):
        for lane0 in range(0, tm, ROUTE_LANES):
            route(h, lane0)
        return carry

    lax.fori_loop(0, PEER_HEADS, body, 0, unroll=8)
    idx_ref[...] = it_ref[...].T.astype(jnp.int32)
    g_ref[...] = gt_ref[...].T


def peer_route(x2d, norm_w, wq_t, keys, tm=256):
    T = x2d.shape[0]
    npick = PEER_HEADS * PEER_TOPK
    return pl.pallas_call(
        _peer_route_kernel,
        grid=(T // tm,),
        in_specs=[
            pl.BlockSpec((tm, D_MODEL), lambda i: (i, 0)),
            pl.BlockSpec((1, D_MODEL), lambda i: (0, 0)),
            pl.BlockSpec((2 * PEER_HEADS * PEER_HALF, D_MODEL), lambda i: (0, 0)),
            pl.BlockSpec((2, PEER_N_KEYS, PEER_HALF), lambda i: (0, 0, 0)),
        ],
        out_specs=[
            pl.BlockSpec((tm, D_MODEL), lambda i: (i, 0)),
            pl.BlockSpec((tm, npick), lambda i: (i, 0)),
            pl.BlockSpec((tm, npick), lambda i: (i, 0)),
        ],
        out_shape=[
            jax.ShapeDtypeStruct((T, D_MODEL), BF16),
            jax.ShapeDtypeStruct((T, npick), jnp.int32),
            jax.ShapeDtypeStruct((T, npick), F32),
        ],
        scratch_shapes=[
            pltpu.VMEM((2 * PEER_HEADS * PEER_HALF, tm), BF16),
            pltpu.VMEM((npick, tm), F32),
            pltpu.VMEM((npick, tm), F32),
        ],
        compiler_params=_cparams(1),
        name="peer_route",
    )(x2d, norm_w, wq_t, keys)


GATE_PITCH = 136
GATE_UNROLL = 64


def _peer_gates_kernel(idx_ref, g_ref, gd_ref, g3_ref):
    tm = idx_ref.shape[0]
    rid = lax.broadcasted_iota(jnp.int32, (PEER_N_KEYS, 128), 0)

    def body(tb, carry):
        t0 = pl.multiple_of(tb * GATE_UNROLL, GATE_UNROLL)
        e = idx_ref[pl.ds(t0, GATE_UNROLL), :]
        g = g_ref[pl.ds(t0, GATE_UNROLL), :]
        i1 = lax.shift_right_logical(e, 7)
        i2 = e & (PEER_N_KEYS - 1)
        for u in range(GATE_UNROLL):
            a_t = jnp.where(rid == i1[u:u + 1], g[u:u + 1], 0.0).astype(BF16)
            b_t = jnp.where(rid == i2[u:u + 1], 1.0, 0.0).astype(BF16)
            gd = lax.dot_general(a_t, b_t, (((1,), (1,)), ((), ())), preferred_element_type=F32)
            g3_ref[pl.ds(t0 + u, PEER_N_KEYS, stride=GATE_PITCH), :] = gd
        return carry

    lax.fori_loop(0, tm // GATE_UNROLL, body, 0)
    for a in range(PEER_N_KEYS):
        gd_ref[:, a * PEER_N_KEYS:(a + 1) * PEER_N_KEYS] = g3_ref[a * GATE_PITCH:a * GATE_PITCH + tm, :].astype(BF16)


def peer_gates(idx, g, tm=128):
    T, npick = idx.shape
    return pl.pallas_call(
        _peer_gates_kernel,
        grid=(T // tm,),
        in_specs=[pl.BlockSpec((tm, npick), lambda i: (i, 0)), pl.BlockSpec((tm, npick), lambda i: (i, 0))],
        out_specs=pl.BlockSpec((tm, PEER_EXPERTS), lambda i: (i, 0)),
        out_shape=jax.ShapeDtypeStruct((T, PEER_EXPERTS), BF16),
        scratch_shapes=[pltpu.VMEM((PEER_N_KEYS * GATE_PITCH, PEER_N_KEYS), F32)],
        compiler_params=_cparams(1),
        name="peer_gates",
    )(idx, g)


def _peer_experts_kernel(x_ref, xn_ref, gd_ref, ut_ref, v_ref, fw_ref, *rest, final_norm, n_first):
    out_refs, acc_ref = rest[:-1], rest[-1]
    i = pl.program_id(0)
    j = pl.program_id(1)

    @pl.when(j == 0)
    def _():
        acc_ref[...] = jnp.zeros_like(acc_ref)

    z = jnp.dot(xn_ref[...], ut_ref[...], preferred_element_type=F32)
    hid = 0.5 * z * (1.0 + lax.erf(z * np.float32(np.sqrt(0.5))))
    w = (gd_ref[...].astype(F32) * hid).astype(BF16)
    acc_ref[...] += jnp.dot(w, v_ref[...], preferred_element_type=F32)

    @pl.when(j == pl.num_programs(1) - 1)
    def _():
        y = x_ref[...] + acc_ref[...]
        if final_norm:
            ms = jnp.mean(y * y, axis=-1, keepdims=True)
            y = (y * lax.rsqrt(ms + RMS_EPS)) * fw_ref[...]
        if n_first is None:
            out_refs[0][...] = y
        else:
            @pl.when(i < n_first)
            def _():
                out_refs[0][...] = y

            @pl.when(i >= n_first)
            def _():
                out_refs[1][...] = y


def peer_experts(x2d, xn, gd, u_t, v, final_w, final_norm, rows_first=None, tm=512, te=2048):
    T = x2d.shape[0]
    if rows_first is None:
        n_first = None
        out_specs = pl.BlockSpec((tm, D_MODEL), lambda i, j: (i, 0))
        out_shape = jax.ShapeDtypeStruct((T, D_MODEL), F32)
    else:
        n_first = rows_first // tm
        out_specs = [
            pl.BlockSpec((tm, D_MODEL), lambda i, j: (jnp.minimum(i, n_first - 1), 0)),
            pl.BlockSpec((tm, D_MODEL), lambda i, j: (jnp.maximum(i - n_first, 0), 0)),
        ]
        out_shape = [jax.ShapeDtypeStruct((rows_first, D_MODEL), F32),
                     jax.ShapeDtypeStruct((T - rows_first, D_MODEL), F32)]
    return pl.pallas_call(
        functools.partial(_peer_experts_kernel, final_norm=final_norm, n_first=n_first),
        grid=(T // tm, PEER_EXPERTS // te),
        in_specs=[
            pl.BlockSpec((tm, D_MODEL), lambda i, j: (i, 0)),
            pl.BlockSpec((tm, D_MODEL), lambda i, j: (i, 0)),
            pl.BlockSpec((tm, te), lambda i, j: (i, j)),
            pl.BlockSpec((D_MODEL, te), lambda i, j: (0, j)),
            pl.BlockSpec((te, D_MODEL), lambda i, j: (j, 0)),
            pl.BlockSpec((1, D_MODEL), lambda i, j: (0, 0)),
        ],
        out_specs=out_specs,
        out_shape=out_shape,
        scratch_shapes=[pltpu.VMEM((tm, D_MODEL), F32)],
        compiler_params=_cparams(2),
        name="peer_experts_final" if final_norm else "peer_experts",
    )(x2d, xn, gd, u_t, v, final_w)


def _t5_bucket(rel):
    half = REL_BUCKETS // 2
    max_exact = half // 2
    n = np.abs(rel)
    large = max_exact + (np.log(np.maximum(n, 1) / max_exact) / np.log(REL_MAX_DISTANCE / max_exact)
                         * (half - max_exact)).astype(np.int32)
    large = np.minimum(large, half - 1)
    return np.where(rel > 0, half, 0) + np.where(n < max_exact, n, large)


def prep_attn_bias(rel_bias, gi, dil):
    i = np.arange(ATT_TQ)[:, None]
    j = np.arange(ATT_TK)[None, :]
    rel = j - ATT_SPAN - i
    bucket = _t5_bucket(rel * dil)
    heads = rel_bias[:, gi * HEADS_PER_GROUP:(gi + 1) * HEADS_PER_GROUP].astype(F32)
    table = jnp.zeros((HEADS_PER_GROUP, ATT_TQ, ATT_TK), F32)
    for b in np.unique(bucket):
        table = table + jnp.where((bucket == b)[None], heads[b][:, None, None], 0.0)
    window = np.abs(rel) <= ATT_SPAN
    after_start = np.broadcast_to(j >= ATT_SPAN, window.shape)
    before_end = np.broadcast_to(j < ATT_SPAN + ATT_TQ, window.shape)
    keep = np.stack([window, window & after_start, window & before_end, window & after_start & before_end])
    return jnp.where(keep[:, None], table[None], NEG)


def _per_head_lanes(t):
    lead = t.shape[:-1]
    t = t.reshape(lead + (2, SSD_GROUPS, 4))
    t = jnp.swapaxes(t, -3, -2)
    t = jnp.broadcast_to(t[..., None], lead + (SSD_GROUPS, 2, 4, DT_REP))
    return t.reshape(lead + (SSD_GROUPS * 128,))


def prep_w_in(w_in_l):
    w_main = jnp.concatenate(
        [w_in_l[:, :OFF_XBC], w_in_l[:, OFF_GATE:], w_in_l[:, OFF_XBC:OFF_DT], w_in_l[:, OFF_Q:OFF_GATE]], axis=1
    ).astype(BF16)
    w_dtx = _per_head_lanes(w_in_l[:, OFF_DT:OFF_Q].astype(BF16))
    return w_main, w_dtx


def prep_ssd_params(dt_bias_l, a_log_l, d_skip_l):
    dtb_g = _per_head_lanes(dt_bias_l.reshape(-1)).reshape(SSD_GROUPS, 1, 128)
    alog_g = _per_head_lanes(a_log_l.reshape(-1)).reshape(SSD_GROUPS, 1, 128)
    dsk_g = jnp.repeat(d_skip_l, SSD_HEAD_DIM).reshape(SSD_GROUPS, 1, 256)
    return dtb_g, alog_g, dsk_g


def trunk_layer(x2d, B, L, biases, norm1_w, w_in, conv_w, conv_b, a_log, dt_bias, d_skip, ssd_norm_w,
                w_ssd_out, w_attn_out, w_out, norm2_w, w_query, sub_keys, expert_u, expert_v, final_w, final_norm,
                rows_first=None):
    w_main, w_dtx = prep_w_in(w_in)
    proj, dtx = in_proj(x2d, norm1_w.reshape(1, -1), w_main, w_dtx)
    proj3 = proj.reshape(B, L, PROJ_COLS)
    yssd = ssd_mixer(proj3, dtx.reshape(B, L, DTX_COLS), conv_w, conv_b.reshape(1, -1),
                     *prep_ssd_params(dt_bias, a_log, d_skip))
    outs, lses = [], []
    for gi, (_, dil) in enumerate(DILATION_GROUPS):
        o, lse = dilated_attention(proj3, biases[gi], gi, dil)
        outs.append(o.reshape(B * L, ATTN_OUT_WIDTH))
        lses.append(lse.reshape(B * L, 128))
    x1 = mix_project(x2d, yssd.reshape(B * L, D_INNER), proj, outs, lses, ssd_norm_w.reshape(1, -1),
                     w_ssd_out.astype(BF16), w_attn_out.astype(BF16), w_out.astype(BF16))
    xn, idx, gate = peer_route(x1, norm2_w.reshape(1, -1), w_query.T.astype(BF16), sub_keys.astype(BF16))
    gd = peer_gates(idx, gate)
    return peer_experts(x1, xn, gd, expert_u.T.astype(BF16), expert_v.astype(BF16), final_w.reshape(1, -1), final_norm,
                        rows_first)


def kernel(x_prompt, x_sample, rel_bias, norm1_w, w_in, conv_w, conv_b, A_log, dt_bias, D_skip, ssd_norm_w,
           w_ssd_out, w_attn_out, w_out, norm2_w, w_query, sub_keys, expert_u, expert_v, final_norm_w):
    nb_prompt = x_prompt.shape[0]
    x = jnp.concatenate([x_prompt, x_sample], axis=0)
    B, L, _ = x.shape
    x2d = x.reshape(B * L, D_MODEL)
    biases = [prep_attn_bias(rel_bias, gi, dil) for gi, (_, dil) in enumerate(DILATION_GROUPS)]
    for l in range(DEPTH):
        last = l == DEPTH - 1
        x2d = trunk_layer(x2d, B, L, biases, norm1_w[l], w_in[l], conv_w[l], conv_b[l], A_log[l], dt_bias[l],
                          D_skip[l], ssd_norm_w[l], w_ssd_out[l], w_attn_out[l], w_out[l], norm2_w[l], w_query[l],
                          sub_keys[l], expert_u[l], expert_v[l], final_norm_w, last,
                          nb_prompt * L if last else None)
    y_prompt, y_sample = x2d
    return y_prompt.reshape(nb_prompt, L, D_MODEL), y_sample.reshape(B - nb_prompt, L, D_MODEL)
```

```python
import functools

import numpy as np
import jax
import jax.numpy as jnp
from jax import lax
from jax.experimental import pallas as pl
from jax.experimental.pallas import tpu as pltpu

F32 = jnp.float32
BF16 = jnp.bfloat16
HIGHEST = lax.Precision.HIGHEST

D_MODEL = 1024
DEPTH = 2
D_INNER = 2048
SSD_HEADS = 32
SSD_HEAD_DIM = 64
SSD_GROUPS = 8
SSD_STATE = 128
CHUNK = 128
D_CONV = 5
GN = SSD_GROUPS * SSD_STATE
CONV_DIM = D_INNER + 2 * GN
ATTN_HEAD_DIM = 64
HEADS_PER_GROUP = 8
DILATION_GROUPS = ((128, 1), (512, 4), (2048, 16))
ATTN_WIDTH = 1536
ATTN_OUT_WIDTH = 512
REL_BUCKETS = 32
REL_MAX_DISTANCE = 1024
PEER_HEADS = 8
PEER_N_KEYS = 128
PEER_EXPERTS = PEER_N_KEYS * PEER_N_KEYS
PEER_HALF = 128
PEER_TOPK = 16
RMS_EPS = 1e-6

OFF_XBC = D_INNER
OFF_DT = OFF_XBC + CONV_DIM
OFF_Q = OFF_DT + 2 * SSD_HEADS

OFF_K = OFF_Q + ATTN_WIDTH
OFF_V = OFF_K + ATTN_WIDTH
OFF_GATE = OFF_V + ATTN_WIDTH

PC_Z = 0
PC_GATE = D_INNER
PC_XBC = PC_GATE + 2 * D_MODEL
PC_Q = PC_XBC + CONV_DIM
PC_K = PC_Q + ATTN_WIDTH
PC_V = PC_K + ATTN_WIDTH
PROJ_COLS = PC_V + ATTN_WIDTH
DT_REP = 16
DTX_COLS = 128

ATT_SPAN = 64
ATT_TQ = 128
ATT_TK = 256
NEG = -1e30

VMEM_LIMIT = 56 * 1024 * 1024


def _cparams(n_axes):
    return pltpu.CompilerParams(
        dimension_semantics=("arbitrary",) * n_axes, vmem_limit_bytes=VMEM_LIMIT
    )


def _inproj_kernel(x_ref, nw_ref, w_ref, wdt_ref, o_ref, dt_ref, xn_ref):
    @pl.when(pl.program_id(1) == 0)
    def _():
        x = x_ref[...]
        ms = jnp.mean(x * x, axis=-1, keepdims=True)
        xn = ((x * lax.rsqrt(ms + RMS_EPS)) * nw_ref[...]).astype(BF16)
        xn_ref[...] = xn
        dt_ref[...] = jnp.dot(xn, wdt_ref[...], preferred_element_type=F32)

    o_ref[...] = jnp.dot(xn_ref[...], w_ref[...], preferred_element_type=F32).astype(BF16)


def in_proj(x2d, norm_w, w_main, w_dtx, tm=1024, tn=2560):
    T = x2d.shape[0]
    nc = w_main.shape[1]
    return pl.pallas_call(
        _inproj_kernel,
        grid=(T // tm, nc // tn),
        in_specs=[
            pl.BlockSpec((tm, D_MODEL), lambda i, j: (i, 0)),
            pl.BlockSpec((1, D_MODEL), lambda i, j: (0, 0)),
            pl.BlockSpec((D_MODEL, tn), lambda i, j: (0, j)),
            pl.BlockSpec((D_MODEL, DTX_COLS), lambda i, j: (0, 0)),
        ],
        out_specs=[
            pl.BlockSpec((tm, tn), lambda i, j: (i, j)),
            pl.BlockSpec((tm, DTX_COLS), lambda i, j: (i, 0)),
        ],
        out_shape=[
            jax.ShapeDtypeStruct((T, nc), BF16),
            jax.ShapeDtypeStruct((T, DTX_COLS), F32),
        ],
        scratch_shapes=[pltpu.VMEM((tm, D_MODEL), BF16)],
        compiler_params=_cparams(2),
        name="in_proj",
    )(x2d, norm_w, w_main, w_dtx)


SSD_HALO = 64


def _split3(x):
    hi = x.astype(BF16).astype(F32)
    r1 = x - hi
    mid = r1.astype(BF16).astype(F32)
    lo = (r1 - mid).astype(BF16).astype(F32)
    return hi, mid, lo


def _ssd_kernel(x_ref, b_ref, c_ref, dt_ref, cwx_ref, cwb_ref, cwc_ref, cbx_ref, cbb_ref, cbc_ref,
                dtb_ref, alog_ref, dsk_ref, y_ref,
                pad_ref, xs_ref, bt_ref, cm_ref, yacc_ref, st_ref, pk_ref, ng_ref):
    L = x_ref.shape[1]
    nch = L // CHUNK
    cw = jnp.concatenate([cwx_ref[...], cwb_ref[...], cwc_ref[...]], axis=1)
    cb = jnp.concatenate([cbx_ref[...], cbb_ref[...], cbc_ref[...]], axis=1)
    dsk = dsk_ref[0]

    zhalo = jnp.zeros((SSD_HALO, 512), BF16)
    pad_ref[0:SSD_HALO, :] = zhalo
    pad_ref[L + SSD_HALO:L + 2 * SSD_HALO, :] = zhalo

    def copy_body(c, carry):
        r = pl.multiple_of(c * CHUNK, CHUNK)
        dst = pl.ds(r + SSD_HALO, CHUNK)
        pad_ref[dst, 0:256] = x_ref[0, pl.ds(r, CHUNK), :]
        pad_ref[dst, 256:384] = b_ref[0, pl.ds(r, CHUNK), :]
        pad_ref[dst, 384:512] = c_ref[0, pl.ds(r, CHUNK), :]
        return carry

    lax.fori_loop(0, nch, copy_body, 0)

    taps = (-2, -1, 1, 2)
    srow = lax.broadcasted_iota(jnp.int32, (len(taps) * CHUNK, CHUNK + 2 * SSD_HALO), 0)
    scol = lax.broadcasted_iota(jnp.int32, (len(taps) * CHUNK, CHUNK + 2 * SSD_HALO), 1)
    soff = jnp.where(srow < CHUNK, taps[0], jnp.where(srow < 2 * CHUNK, taps[1], jnp.where(srow < 3 * CHUNK, taps[2], taps[3])))
    shift_mat = (scol == (srow & (CHUNK - 1)) + SSD_HALO + soff).astype(BF16)

    def conv_body(c, carry):
        r = pl.multiple_of(c * CHUNK, CHUNK)
        win = pad_ref[pl.ds(r, CHUNK + 2 * SSD_HALO), :]
        sh = jnp.dot(shift_mat, win, preferred_element_type=F32)
        acc = win[SSD_HALO:SSD_HALO + CHUNK, :].astype(F32) * cw[2:3, :] + cb
        for i, w in enumerate((0, 1, 3, 4)):
            acc = acc + sh[i * CHUNK:(i + 1) * CHUNK, :] * cw[w:w + 1, :]
        act = acc * jax.nn.sigmoid(acc)
        xs = act[:, 0:256]
        xs_ref[pl.ds(r, CHUNK), :] = xs
        yacc_ref[pl.ds(r, CHUNK), :] = xs * dsk
        bt_ref[c] = act[:, 256:384].T.astype(BF16)
        cm_ref[pl.ds(r, CHUNK), :] = act[:, 384:512].astype(BF16)
        return carry

    lax.fori_loop(0, nch, conv_body, 0, unroll=4)

    st_ref[...] = jnp.zeros_like(st_ref)

    row = lax.broadcasted_iota(jnp.int32, (CHUNK, CHUNK), 0)
    col = lax.broadcasted_iota(jnp.int32, (CHUNK, CHUNK), 1)
    lane_lt64 = col < 64
    a_neg = -jnp.exp(alog_ref[0])
    dtb = dtb_ref[0]
    erow = lax.broadcasted_iota(jnp.int32, (CHUNK, 2 * CHUNK), 0)
    ecol = lax.broadcasted_iota(jnp.int32, (CHUNK, 2 * CHUNK), 1)

    def selectors(d):
        rel = erow - (64 * d + DT_REP * lax.shift_right_logical(ecol, 6))
        sel_cs = ((rel >= 0) & (rel < 3)).astype(BF16)
        sel_dt = ((rel >= 6) & (rel < 9)).astype(BF16)
        return sel_cs, sel_dt

    sels = (selectors(0), selectors(1))
    tri3_pre = jnp.concatenate([(row <= col).astype(BF16)] * 3, axis=0)
    tri3_suf = jnp.concatenate([(row >= col).astype(BF16)] * 3, axis=0)
    r8 = lax.broadcasted_iota(jnp.int32, (8, CHUNK), 0)
    sel_pack = ((col == DT_REP * (row & 7) + lax.shift_right_logical(row, 3)) & (row < 72)).astype(BF16)
    head_rows = [((row >= DT_REP * k) & (row < DT_REP * k + 6)).astype(BF16) for k in range(8)]
    half_lo = lane_lt64.astype(BF16)
    half_hi = 1 - half_lo
    ones8 = jnp.ones((8, CHUNK), F32)

    raw = dt_ref[0, 0]
    dt_all = jnp.concatenate([jax.nn.softplus(raw[:, c * CHUNK:(c + 1) * CHUNK] + dtb) for c in range(nch)], axis=0)
    a_all = dt_all * jnp.concatenate([a_neg] * nch, axis=0)
    a3 = jnp.concatenate([t.astype(BF16) for t in _split3(a_all)], axis=1)
    rk = lax.broadcasted_iota(jnp.int32, (8 * nch, CHUNK), 0) & 7
    cs_all = jnp.where(rk < 4,
                       jnp.dot(a3, tri3_pre, preferred_element_type=F32),
                       jnp.dot(a3, tri3_suf, preferred_element_type=F32))
    c_parts = _split3(cs_all)
    d_parts = _split3(dt_all)
    for c in range(nch):
        sl = slice(8 * c, 8 * c + 8)
        stack = jnp.concatenate([p[sl] for p in c_parts] + [ones8] * 3 + [p[sl] for p in d_parts]
                                + [jnp.zeros((CHUNK - 72, CHUNK), F32)], axis=0)
        pk_ref[c] = jnp.dot(stack.T.astype(BF16), sel_pack, preferred_element_type=F32).astype(BF16)
        stack_n = jnp.concatenate([ones8] * 3 + [-p[sl] for p in c_parts]
                                  + [jnp.zeros((CHUNK - 48, CHUNK), F32)], axis=0)
        negs = jnp.dot(stack_n.T.astype(BF16), sel_pack, preferred_element_type=F32)
        negs_t = negs.T.astype(BF16)
        for d in range(2):
            ng_ref[c, d] = jnp.concatenate([negs_t * head_rows[4 * d + h] for h in range(4)], axis=1)

    def one_pass(d, c):
        r = pl.multiple_of(c * CHUNK, CHUNK)
        rows = pl.ds(r, CHUNK)
        mask = (row >= col) if d == 0 else (row <= col)
        last = CHUNK - 1 if d == 0 else 0
        packed = pk_ref[c]
        sel_cs, sel_dt = sels[d]
        cs2 = jnp.dot(packed, sel_cs, preferred_element_type=F32)
        dt2 = jnp.dot(packed, sel_dt, preferred_element_type=F32)
        diff = jnp.dot(packed, ng_ref[c, d], preferred_element_type=F32)
        tot2 = cs2[last:last + 1, :]
        ein2 = jnp.exp(cs2)
        dd2 = jnp.exp(tot2 - cs2) * dt2
        cd2 = jnp.exp(tot2)
        cmat = cm_ref[rows, :]
        btc = bt_ref[c]
        cbm = jnp.dot(cmat, btc, preferred_element_type=F32)
        xs = xs_ref[rows, :]
        xg = (xs * dt2).astype(BF16)
        xdec = (xs * dd2).astype(BF16)
        ydiag = []
        for p in range(2):
            cols = slice(128 * p, 128 * p + 128)
            gs = []
            for hh in range(2):
                h = 2 * p + hh
                seg = jnp.where(mask, jnp.exp(diff[:, 128 * h:128 * h + 128]), 0.0)
                gs.append((cbm * seg).astype(BF16))
            x2 = xg[:, cols]
            ydiag.append(jnp.dot(jnp.concatenate(gs, axis=1), jnp.concatenate([x2 * half_lo, x2 * half_hi], axis=0),
                                 preferred_element_type=F32))
        st = st_ref[d]
        yoff = jnp.dot(cmat, st.astype(BF16), preferred_element_type=F32) * ein2
        yacc_ref[rows, :] += jnp.concatenate(ydiag, axis=1) + yoff
        st_ref[d] = st * cd2 + jnp.dot(btc, xdec, preferred_element_type=F32)

    per_trip = 4

    def main_body(i, carry):
        for u in range(per_trip):
            one_pass(0, per_trip * i + u)
            one_pass(1, nch - 1 - per_trip * i - u)
        return carry

    lax.fori_loop(0, nch // per_trip, main_body, 0)

    def out_body(c, carry):
        rows = pl.ds(pl.multiple_of(c * CHUNK, CHUNK), CHUNK)
        y_ref[0, rows, :] = yacc_ref[rows, :].astype(BF16)
        return carry

    lax.fori_loop(0, nch, out_body, 0)


def ssd_mixer(proj3, dtx3, conv_w, conv_b, dtb_g, alog_g, dsk_g):
    B, L, _ = proj3.shape
    xb = PC_XBC // 256
    bb = (PC_XBC + D_INNER) // 128
    cbk = (PC_XBC + D_INNER + GN) // 128
    return pl.pallas_call(
        _ssd_kernel,
        grid=(B, SSD_GROUPS),
        in_specs=[
            pl.BlockSpec((1, L, 256), lambda b, g: (b, 0, xb + g)),
            pl.BlockSpec((1, L, 128), lambda b, g: (b, 0, bb + g)),
            pl.BlockSpec((1, L, 128), lambda b, g: (b, 0, cbk + g)),
            pl.BlockSpec((1, 1, 8, L), lambda b, g: (b, g, 0, 0)),
            pl.BlockSpec((D_CONV, 256), lambda b, g: (0, g)),
            pl.BlockSpec((D_CONV, 128), lambda b, g: (0, D_INNER // 128 + g)),
            pl.BlockSpec((D_CONV, 128), lambda b, g: (0, (D_INNER + GN) // 128 + g)),
            pl.BlockSpec((1, 256), lambda b, g: (0, g)),
            pl.BlockSpec((1, 128), lambda b, g: (0, D_INNER // 128 + g)),
            pl.BlockSpec((1, 128), lambda b, g: (0, (D_INNER + GN) // 128 + g)),
            pl.BlockSpec((1, 8, CHUNK), lambda b, g: (g, 0, 0)),
            pl.BlockSpec((1, 8, CHUNK), lambda b, g: (g, 0, 0)),
            pl.BlockSpec((1, 1, 256), lambda b, g: (g, 0, 0)),
        ],
        out_specs=pl.BlockSpec((1, L, 256), lambda b, g: (b, 0, g)),
        out_shape=jax.ShapeDtypeStruct((B, L, D_INNER), BF16),
        scratch_shapes=[
            pltpu.VMEM((L + 2 * SSD_HALO, 512), BF16),
            pltpu.VMEM((L, 256), F32),
            pltpu.VMEM((L // CHUNK, SSD_STATE, CHUNK), BF16),
            pltpu.VMEM((L, SSD_STATE), BF16),
            pltpu.VMEM((L, 256), F32),
            pltpu.VMEM((2, SSD_STATE, 256), F32),
            pltpu.VMEM((L // CHUNK, CHUNK, 128), BF16),
            pltpu.VMEM((L // CHUNK, 2, 128, 4 * CHUNK), BF16),
        ],
        compiler_params=_cparams(2),
        name="ssd_mixer",
    )(proj3, proj3, proj3, dtx3, conv_w, conv_w, conv_w, conv_b, conv_b, conv_b, dtb_g, alog_g, dsk_g)


def _attn_kernel(q_ref, k_ref, v_ref, bias_ref, o_ref, lse_ref, kp_ref, vp_ref, *stage, dil):
    L = q_ref.shape[1]
    M = L // dil
    nt = M // ATT_TQ
    width = q_ref.shape[2]
    nslab = width // 128
    zpad = jnp.zeros((ATT_SPAN, width), BF16)
    kp_ref[0:ATT_SPAN, :] = zpad
    kp_ref[M + ATT_SPAN:M + 2 * ATT_SPAN, :] = zpad
    vp_ref[0:ATT_SPAN, :] = zpad
    vp_ref[M + ATT_SPAN:M + 2 * ATT_SPAN, :] = zpad

    if dil > 1:
        qf_ref, kf_ref, vf_ref, of_ref, lf_ref = stage

        def stage_body(c, carry):
            rows = pl.ds(pl.multiple_of(c * ATT_TQ, ATT_TQ), ATT_TQ)
            for s in range(nslab):
                cols = slice(128 * s, 128 * s + 128)
                qf_ref[s, rows, :] = q_ref[0, rows, cols].astype(F32)
                kf_ref[s, rows, :] = k_ref[0, rows, cols].astype(F32)
                vf_ref[s, rows, :] = v_ref[0, rows, cols].astype(F32)
            return carry

        lax.fori_loop(0, L // ATT_TQ, stage_body, 0)
    else:
        qf_ref = kf_ref = vf_ref = of_ref = lf_ref = None

    def sub_rows(r, t):
        return pl.ds(r + t * (ATT_TQ * dil), ATT_TQ, stride=dil)

    def load_tile(ref, ref_f32, r, t):
        if dil == 1:
            return ref[0, pl.ds(pl.multiple_of(t * ATT_TQ, ATT_TQ), ATT_TQ), :]
        return jnp.concatenate([ref_f32[s, sub_rows(r, t), :] for s in range(nslab)], axis=1).astype(BF16)

    lane = lax.broadcasted_iota(jnp.int32, (ATT_TQ, 128), 1)
    lane_lt64 = lane < 64

    def residue_body(r, carry):
        def copy_body(t, c2):
            dst = pl.ds(pl.multiple_of(t * ATT_TQ, ATT_TQ) + ATT_SPAN, ATT_TQ)
            kp_ref[dst, :] = load_tile(k_ref, kf_ref, r, t)
            vp_ref[dst, :] = load_tile(v_ref, vf_ref, r, t)
            return c2

        lax.fori_loop(0, nt, copy_body, 0)

        def tile_body(t, c2):
            r0 = pl.multiple_of(t * ATT_TQ, ATT_TQ)
            q = load_tile(q_ref, qf_ref, r, t) * jnp.asarray(ATTN_HEAD_DIM ** -0.5, BF16)
            kw = kp_ref[pl.ds(r0, ATT_TK), :]
            vw = vp_ref[pl.ds(r0, ATT_TK), :]
            edge = jnp.where(t == 0, 1, 0) + jnp.where(t == nt - 1, 2, 0)
            lse_tile = jnp.zeros((ATT_TQ, 128), F32)
            outs = []
            for p in range(HEADS_PER_GROUP // 2):
                cols = slice(128 * p, 128 * p + 128)
                q2, k2, v2 = q[:, cols], kw[:, cols], vw[:, cols]
                pv, dens = [], []
                for hh in range(2):
                    h = 2 * p + hh
                    qm = jnp.where(lane_lt64 if hh == 0 else jnp.logical_not(lane_lt64), q2, jnp.zeros_like(q2))
                    s = lax.dot_general(qm, k2, (((1,), (1,)), ((), ())), preferred_element_type=F32)
                    s = s + bias_ref[edge, h]
                    m = jnp.max(s, axis=-1, keepdims=True)
                    pexp = jnp.exp(s - m)
                    den = jnp.sum(pexp, axis=-1, keepdims=True)
                    pv.append(jnp.dot(pexp.astype(BF16), v2, preferred_element_type=F32))
                    dens.append(den)
                    lse_tile = jnp.where(lane == h, m + jnp.log(den), lse_tile)
                outs.append(jnp.where(lane_lt64, pv[0], pv[1]) / jnp.where(lane_lt64, dens[0], dens[1]))
            if dil == 1:
                o_ref[0, pl.ds(r0, ATT_TQ), :] = jnp.concatenate(outs, axis=1).astype(BF16)
                lse_ref[0, pl.ds(r0, ATT_TQ), :] = lse_tile
            else:
                for p, o2 in enumerate(outs):
                    of_ref[p, sub_rows(r, t), :] = o2
                lf_ref[sub_rows(r, t), :] = lse_tile
            return c2

        lax.fori_loop(0, nt, tile_body, 0, unroll=2 if nt > 1 else 1)
        return carry

    if dil == 1:
        residue_body(0, 0)
    else:
        lax.fori_loop(0, dil, residue_body, 0, unroll=2 if nt == 1 else 1)

        def out_body(c, carry):
            rows = pl.ds(pl.multiple_of(c * ATT_TQ, ATT_TQ), ATT_TQ)
            o_ref[0, rows, :] = jnp.concatenate([of_ref[s, rows, :] for s in range(nslab)], axis=1).astype(BF16)
            lse_ref[0, rows, :] = lf_ref[rows, :]
            return carry

        lax.fori_loop(0, L // ATT_TQ, out_body, 0)


def dilated_attention(proj3, bias_g, gi, dil):
    B, L, _ = proj3.shape
    M = L // dil
    width = HEADS_PER_GROUP * ATTN_HEAD_DIM
    nslab = width // 128
    qb, kb, vb = PC_Q // width + gi, PC_K // width + gi, PC_V // width + gi
    scratch = [
        pltpu.VMEM((M + 2 * ATT_SPAN, width), BF16),
        pltpu.VMEM((M + 2 * ATT_SPAN, width), BF16),
    ]
    if dil > 1:
        scratch += [pltpu.VMEM((nslab, L, 128), F32)] * 4 + [pltpu.VMEM((L, 128), F32)]
    return pl.pallas_call(
        functools.partial(_attn_kernel, dil=dil),
        grid=(B,),
        in_specs=[
            pl.BlockSpec((1, L, width), lambda b: (b, 0, qb)),
            pl.BlockSpec((1, L, width), lambda b: (b, 0, kb)),
            pl.BlockSpec((1, L, width), lambda b: (b, 0, vb)),
            pl.BlockSpec((4, HEADS_PER_GROUP, ATT_TQ, ATT_TK), lambda b: (0, 0, 0, 0)),
        ],
        out_specs=[
            pl.BlockSpec((1, L, width), lambda b: (b, 0, 0)),
            pl.BlockSpec((1, L, 128), lambda b: (b, 0, 0)),
        ],
        out_shape=[
            jax.ShapeDtypeStruct((B, L, width), BF16),
            jax.ShapeDtypeStruct((B, L, 128), F32),
        ],
        scratch_shapes=scratch,
        compiler_params=_cparams(1),
        name=f"dilated_attention_d{dil}",
    )(proj3, proj3, proj3, bias_g)


def _mix_kernel(x_ref, y_ref, z_ref, g_ref, o0_ref, o1_ref, o2_ref, l0_ref, l1_ref, l2_ref,
                nw_ref, wssd_ref, watt_ref, wout_ref, out_ref):
    z = z_ref[...].astype(F32)
    yz = y_ref[...].astype(F32) * (z * jax.nn.sigmoid(z))
    ms = jnp.mean(yz * yz, axis=-1, keepdims=True)
    yn = ((yz * lax.rsqrt(ms + RMS_EPS)) * nw_ref[...]).astype(BF16)
    pa = jnp.dot(yn, wssd_ref[...], preferred_element_type=F32)

    ls = [l0_ref[...], l1_ref[...], l2_ref[...]]
    mx = jnp.maximum(jnp.maximum(ls[0], ls[1]), ls[2])
    es = [jnp.exp(l - mx) for l in ls]
    den = es[0] + es[1] + es[2]
    erow = lax.broadcasted_iota(jnp.int32, (128, ATTN_OUT_WIDTH), 0)
    ecol = lax.broadcasted_iota(jnp.int32, (128, ATTN_OUT_WIDTH), 1)
    expand = (erow == ecol // ATTN_HEAD_DIM).astype(F32)
    yatt = None
    for e, o_ref in zip(es, (o0_ref, o1_ref, o2_ref)):
        wexp = jnp.dot(e / den, expand, precision=HIGHEST, preferred_element_type=F32)
        term = wexp * o_ref[...].astype(F32)
        yatt = term if yatt is None else yatt + term
    pb = jnp.dot(yatt.astype(BF16), watt_ref[...], preferred_element_type=F32)

    gates = jax.nn.sigmoid(g_ref[...].astype(F32))
    merged = gates[:, :D_MODEL] * pa + gates[:, D_MODEL:] * pb
    out_ref[...] = x_ref[...] + jnp.dot(merged.astype(BF16), wout_ref[...], preferred_element_type=F32)


def mix_project(x2d, yssd, proj, outs, lses, norm_w, w_ssd, w_att, w_out, tm=512):
    T = x2d.shape[0]
    row = lambda i: (i, 0)
    const = lambda i: (0, 0)
    return pl.pallas_call(
        _mix_kernel,
        grid=(T // tm,),
        in_specs=[
            pl.BlockSpec((tm, D_MODEL), row),
            pl.BlockSpec((tm, D_INNER), row),
            pl.BlockSpec((tm, D_INNER), lambda i: (i, PC_Z // D_INNER)),
            pl.BlockSpec((tm, 2 * D_MODEL), lambda i: (i, PC_GATE // (2 * D_MODEL))),
            pl.BlockSpec((tm, ATTN_OUT_WIDTH), row),
            pl.BlockSpec((tm, ATTN_OUT_WIDTH), row),
            pl.BlockSpec((tm, ATTN_OUT_WIDTH), row),
            pl.BlockSpec((tm, 128), row),
            pl.BlockSpec((tm, 128), row),
            pl.BlockSpec((tm, 128), row),
            pl.BlockSpec((1, D_INNER), const),
            pl.BlockSpec((D_INNER, D_MODEL), const),
            pl.BlockSpec((ATTN_OUT_WIDTH, D_MODEL), const),
            pl.BlockSpec((D_MODEL, D_MODEL), const),
        ],
        out_specs=pl.BlockSpec((tm, D_MODEL), row),
        out_shape=jax.ShapeDtypeStruct((T, D_MODEL), F32),
        compiler_params=_cparams(1),
        name="mix_project",
    )(x2d, yssd, proj, proj, outs[0], outs[1], outs[2], lses[0], lses[1], lses[2], norm_w, w_ssd, w_att, w_out)


ROUTE_LANES = 128
N_CAND = 80


def _extract_top(vals, ids, n, big):
    out_v, out_i = [], []
    for _ in range(n):
        m = jnp.max(vals, axis=0, keepdims=True)
        pick = jnp.min(jnp.where(vals == m, ids, big), axis=0, keepdims=True)
        out_v.append(m)
        out_i.append(pick)
        vals = jnp.where(ids == pick, -jnp.inf, vals)
    return out_v, out_i


def _extract_top_paired(vals, ids, n, big):
    half = vals.shape[0] // 2
    first = vals[:half] >= vals[half:]
    win_v = jnp.where(first, vals[:half], vals[half:])
    win_i = jnp.where(first, ids[:half], ids[half:])
    los_v = jnp.where(first, vals[half:], vals[:half])
    los_i = jnp.where(first, ids[half:], ids[:half])
    out_v, out_i = [], []
    for _ in range(n):
        m = jnp.max(win_v, axis=0, keepdims=True)
        pick = jnp.min(jnp.where(win_v == m, win_i, big), axis=0, keepdims=True)
        out_v.append(m)
        out_i.append(pick)
        hit = win_i == pick
        win_v = jnp.where(hit, los_v, win_v)
        win_i = jnp.where(hit, los_i, win_i)
        los_v = jnp.where(hit, -jnp.inf, los_v)
    return out_v, out_i


def _stack_rows(rows):
    n = len(rows)
    rid = lax.broadcasted_iota(jnp.int32, (n, rows[0].shape[1]), 0)
    out = jnp.broadcast_to(rows[0], rid.shape)
    for k in range(1, n):
        out = jnp.where(rid == k, rows[k], out)
    return out


def _peer_route_kernel(x_ref, nw_ref, wq_ref, keys_ref, xn_ref, idx_ref, g_ref, qt_ref, it_ref, gt_ref):
    tm = x_ref.shape[0]
    x = x_ref[...]
    ms = jnp.mean(x * x, axis=-1, keepdims=True)
    xn = ((x * lax.rsqrt(ms + RMS_EPS)) * nw_ref[...]).astype(BF16)
    xn_ref[...] = xn
    qt_ref[...] = lax.dot_general(wq_ref[...], xn, (((1,), (1,)), ((), ())),
                                  preferred_element_type=F32).astype(BF16)

    key_id = lax.broadcasted_iota(jnp.int32, (PEER_N_KEYS, ROUTE_LANES), 0).astype(F32)
    r80 = lax.broadcasted_iota(jnp.int32, (N_CAND, ROUTE_LANES), 0)
    rank_a = jnp.where(r80 < 16, 0, jnp.where(r80 < 72, 1 + lax.shift_right_logical(r80 - 16, 3), r80 - 64))
    rank_b = jnp.where(r80 < 16, r80, jnp.where(r80 < 72, (r80 - 16) & 7, 0))
    code = (rank_a * PEER_TOPK + rank_b).astype(F32)

    def route(h, lane0):
        lanes = slice(lane0, lane0 + ROUTE_LANES)
        sv, si = [], []
        for s in range(2):
            qhs = qt_ref[pl.ds(pl.multiple_of((2 * h + s) * PEER_HALF, PEER_HALF), PEER_HALF), lanes]
            sc = jnp.dot(keys_ref[s], qhs, preferred_element_type=F32)
            v, i = _extract_top_paired(sc, key_id, PEER_TOPK, float(PEER_N_KEYS))
            sv.append(v)
            si.append(i)
        sv1 = _stack_rows(sv[1])
        si1 = _stack_rows(si[1])
        cand = [sv[0][0] + sv1]
        eid = [si[0][0] * PEER_N_KEYS + si1]
        for a in range(1, 8):
            cand.append(sv[0][a] + sv1[0:8])
            eid.append(si[0][a] * PEER_N_KEYS + si1[0:8])
        cand.append(_stack_rows(sv[0][8:16]) + sv1[0:1])
        eid.append(_stack_rows(si[0][8:16]) * PEER_N_KEYS + si1[0:1])
        cand = jnp.concatenate(cand, axis=0)
        eid = jnp.concatenate(eid, axis=0)
        cv, cc = _extract_top(cand, code * PEER_EXPERTS + eid, PEER_TOPK, float(256 * PEER_EXPERTS))
        picked = [c_k - jnp.floor(c_k * (1.0 / PEER_EXPERTS)) * PEER_EXPERTS for c_k in cc]
        cvs = _stack_rows(cv)
        ex = jnp.exp(cvs - cvs[0:1])
        gate = ex / jnp.sum(ex, axis=0, keepdims=True)
        rows = pl.ds(pl.multiple_of(h * PEER_TOPK, PEER_TOPK), PEER_TOPK)
        it_ref[rows, lanes] = _stack_rows(picked)
        gt_ref[rows, lanes] = gate

    def body(h, carry):
        for lane0 in range(0, tm, ROUTE_LANES):
            route(h, lane0)
        return carry

    lax.fori_loop(0, PEER_HEADS, body, 0, unroll=8)
    idx_ref[...] = it_ref[...].T.astype(jnp.int32)
    g_ref[...] = gt_ref[...].T


def peer_route(x2d, norm_w, wq_t, keys, tm=256):
    T = x2d.shape[0]
    npick = PEER_HEADS * PEER_TOPK
    return pl.pallas_call(
        _peer_route_kernel,
        grid=(T // tm,),
        in_specs=[
            pl.BlockSpec((tm, D_MODEL), lambda i: (i, 0)),
            pl.BlockSpec((1, D_MODEL), lambda i: (0, 0)),
            pl.BlockSpec((2 * PEER_HEADS * PEER_HALF, D_MODEL), lambda i: (0, 0)),
            pl.BlockSpec((2, PEER_N_KEYS, PEER_HALF), lambda i: (0, 0, 0)),
        ],
        out_specs=[
            pl.BlockSpec((tm, D_MODEL), lambda i: (i, 0)),
            pl.BlockSpec((tm, npick), lambda i: (i, 0)),
            pl.BlockSpec((tm, npick), lambda i: (i, 0)),
        ],
        out_shape=[
            jax.ShapeDtypeStruct((T, D_MODEL), BF16),
            jax.ShapeDtypeStruct((T, npick), jnp.int32),
            jax.ShapeDtypeStruct((T, npick), F32),
        ],
        scratch_shapes=[
            pltpu.VMEM((2 * PEER_HEADS * PEER_HALF, tm), BF16),
            pltpu.VMEM((npick, tm), F32),
            pltpu.VMEM((npick, tm), F32),
        ],
        compiler_params=_cparams(1),
        name="peer_route",
    )(x2d, norm_w, wq_t, keys)


GATE_PITCH = 136
GATE_UNROLL = 64


def _peer_gates_kernel(idx_ref, g_ref, gd_ref, g3_ref):
    tm = idx_ref.shape[0]
    rid = lax.broadcasted_iota(jnp.int32, (PEER_N_KEYS, 128), 0)

    def body(tb, carry):
        t0 = pl.multiple_of(tb * GATE_UNROLL, GATE_UNROLL)
        e = idx_ref[pl.ds(t0, GATE_UNROLL), :]
        g = g_ref[pl.ds(t0, GATE_UNROLL), :]
        i1 = lax.shift_right_logical(e, 7)
        i2 = e & (PEER_N_KEYS - 1)
        for u in range(GATE_UNROLL):
            a_t = jnp.where(rid == i1[u:u + 1], g[u:u + 1], 0.0).astype(BF16)
            b_t = jnp.where(rid == i2[u:u + 1], 1.0, 0.0).astype(BF16)
            gd = lax.dot_general(a_t, b_t, (((1,), (1,)), ((), ())), preferred_element_type=F32)
            g3_ref[pl.ds(t0 + u, PEER_N_KEYS, stride=GATE_PITCH), :] = gd
        return carry

    lax.fori_loop(0, tm // GATE_UNROLL, body, 0)
    for a in range(PEER_N_KEYS):
        gd_ref[:, a * PEER_N_KEYS:(a + 1) * PEER_N_KEYS] = g3_ref[a * GATE_PITCH:a * GATE_PITCH + tm, :].astype(BF16)


def peer_gates(idx, g, tm=128):
    T, npick = idx.shape
    return pl.pallas_call(
        _peer_gates_kernel,
        grid=(T // tm,),
        in_specs=[pl.BlockSpec((tm, npick), lambda i: (i, 0)), pl.BlockSpec((tm, npick), lambda i: (i, 0))],
        out_specs=pl.BlockSpec((tm, PEER_EXPERTS), lambda i: (i, 0)),
        out_shape=jax.ShapeDtypeStruct((T, PEER_EXPERTS), BF16),
        scratch_shapes=[pltpu.VMEM((PEER_N_KEYS * GATE_PITCH, PEER_N_KEYS), F32)],
        compiler_params=_cparams(1),
        name="peer_gates",
    )(idx, g)


def _peer_experts_kernel(x_ref, xn_ref, gd_ref, ut_ref, v_ref, fw_ref, *rest, final_norm, n_first):
    out_refs, acc_ref = rest[:-1], rest[-1]
    i = pl.program_id(0)
    j = pl.program_id(1)

    @pl.when(j == 0)
    def _():
        acc_ref[...] = jnp.zeros_like(acc_ref)

    z = jnp.dot(xn_ref[...], ut_ref[...], preferred_element_type=F32)
    hid = 0.5 * z * (1.0 + lax.erf(z * np.float32(np.sqrt(0.5))))
    w = (gd_ref[...].astype(F32) * hid).astype(BF16)
    acc_ref[...] += jnp.dot(w, v_ref[...], preferred_element_type=F32)

    @pl.when(j == pl.num_programs(1) - 1)
    def _():
        y = x_ref[...] + acc_ref[...]
        if final_norm:
            ms = jnp.mean(y * y, axis=-1, keepdims=True)
            y = (y * lax.rsqrt(ms + RMS_EPS)) * fw_ref[...]
        if n_first is None:
            out_refs[0][...] = y
        else:
            @pl.when(i < n_first)
            def _():
                out_refs[0][...] = y

            @pl.when(i >= n_first)
            def _():
                out_refs[1][...] = y


def peer_experts(x2d, xn, gd, u_t, v, final_w, final_norm, rows_first=None, tm=512, te=2048):
    T = x2d.shape[0]
    if rows_first is None:
        n_first = None
        out_specs = pl.BlockSpec((tm, D_MODEL), lambda i, j: (i, 0))
        out_shape = jax.ShapeDtypeStruct((T, D_MODEL), F32)
    else:
        n_first = rows_first // tm
        out_specs = [
            pl.BlockSpec((tm, D_MODEL), lambda i, j: (jnp.minimum(i, n_first - 1), 0)),
            pl.BlockSpec((tm, D_MODEL), lambda i, j: (jnp.maximum(i - n_first, 0), 0)),
        ]
        out_shape = [jax.ShapeDtypeStruct((rows_first, D_MODEL), F32),
                     jax.ShapeDtypeStruct((T - rows_first, D_MODEL), F32)]
    return pl.pallas_call(
        functools.partial(_peer_experts_kernel, final_norm=final_norm, n_first=n_first),
        grid=(T // tm, PEER_EXPERTS // te),
        in_specs=[
            pl.BlockSpec((tm, D_MODEL), lambda i, j: (i, 0)),
            pl.BlockSpec((tm, D_MODEL), lambda i, j: (i, 0)),
            pl.BlockSpec((tm, te), lambda i, j: (i, j)),
            pl.BlockSpec((D_MODEL, te), lambda i, j: (0, j)),
            pl.BlockSpec((te, D_MODEL), lambda i, j: (j, 0)),
            pl.BlockSpec((1, D_MODEL), lambda i, j: (0, 0)),
        ],
        out_specs=out_specs,
        out_shape=out_shape,
        scratch_shapes=[pltpu.VMEM((tm, D_MODEL), F32)],
        compiler_params=_cparams(2),
        name="peer_experts_final" if final_norm else "peer_experts",
    )(x2d, xn, gd, u_t, v, final_w)


def _t5_bucket(rel):
    half = REL_BUCKETS // 2
    max_exact = half // 2
    n = np.abs(rel)
    large = max_exact + (np.log(np.maximum(n, 1) / max_exact) / np.log(REL_MAX_DISTANCE / max_exact)
                         * (half - max_exact)).astype(np.int32)
    large = np.minimum(large, half - 1)
    return np.where(rel > 0, half, 0) + np.where(n < max_exact, n, large)


def prep_attn_bias(rel_bias, gi, dil):
    i = np.arange(ATT_TQ)[:, None]
    j = np.arange(ATT_TK)[None, :]
    rel = j - ATT_SPAN - i
    bucket = _t5_bucket(rel * dil)
    heads = rel_bias[:, gi * HEADS_PER_GROUP:(gi + 1) * HEADS_PER_GROUP].astype(F32)
    table = jnp.zeros((HEADS_PER_GROUP, ATT_TQ, ATT_TK), F32)
    for b in np.unique(bucket):
        table = table + jnp.where((bucket == b)[None], heads[b][:, None, None], 0.0)
    window = np.abs(rel) <= ATT_SPAN
    after_start = np.broadcast_to(j >= ATT_SPAN, window.shape)
    before_end = np.broadcast_to(j < ATT_SPAN + ATT_TQ, window.shape)
    keep = np.stack([window, window & after_start, window & before_end, window & after_start & before_end])
    return jnp.where(keep[:, None], table[None], NEG)


def _group_major(t):
    lead = t.shape[:-1]
    t = jnp.swapaxes(t.reshape(lead + (2, SSD_GROUPS, 4)), -3, -2)
    return t.reshape(lead + (SSD_GROUPS, 8))


def prep_w_in(w_in_l):
    w_main = jnp.concatenate(
        [w_in_l[:, :OFF_XBC], w_in_l[:, OFF_GATE:], w_in_l[:, OFF_XBC:OFF_DT], w_in_l[:, OFF_Q:OFF_GATE]], axis=1
    ).astype(BF16)
    w_dt = _group_major(w_in_l[:, OFF_DT:OFF_Q].astype(BF16)).reshape(D_MODEL, 2 * SSD_HEADS)
    return w_main, jnp.pad(w_dt, ((0, 0), (0, DTX_COLS - 2 * SSD_HEADS)))


def prep_ssd_params(dt_bias_l, a_log_l, d_skip_l):
    lanes = (SSD_GROUPS, 8, CHUNK)
    dtb_g = jnp.broadcast_to(_group_major(dt_bias_l.reshape(-1))[..., None], lanes)
    alog_g = jnp.broadcast_to(_group_major(a_log_l.reshape(-1))[..., None], lanes)
    dsk_g = jnp.repeat(d_skip_l, SSD_HEAD_DIM).reshape(SSD_GROUPS, 1, 256)
    return dtb_g, alog_g, dsk_g


def dt_by_group(dt2d, B, L):
    return jnp.transpose(dt2d[:, :2 * SSD_HEADS].reshape(B, L, SSD_GROUPS, 8), (0, 2, 3, 1))


def trunk_layer(x2d, B, L, biases, norm1_w, w_in, conv_w, conv_b, a_log, dt_bias, d_skip, ssd_norm_w,
                w_ssd_out, w_attn_out, w_out, norm2_w, w_query, sub_keys, expert_u, expert_v, final_w, final_norm,
                rows_first=None):
    w_main, w_dtx = prep_w_in(w_in)
    proj, dtx = in_proj(x2d, norm1_w.reshape(1, -1), w_main, w_dtx)
    proj3 = proj.reshape(B, L, PROJ_COLS)
    yssd = ssd_mixer(proj3, dt_by_group(dtx, B, L), conv_w, conv_b.reshape(1, -1),
                     *prep_ssd_params(dt_bias, a_log, d_skip))
    outs, lses = [], []
    for gi, (_, dil) in enumerate(DILATION_GROUPS):
        o, lse = dilated_attention(proj3, biases[gi], gi, dil)
        outs.append(o.reshape(B * L, ATTN_OUT_WIDTH))
        lses.append(lse.reshape(B * L, 128))
    x1 = mix_project(x2d, yssd.reshape(B * L, D_INNER), proj, outs, lses, ssd_norm_w.reshape(1, -1),
                     w_ssd_out.astype(BF16), w_attn_out.astype(BF16), w_out.astype(BF16))
    xn, idx, gate = peer_route(x1, norm2_w.reshape(1, -1), w_query.T.astype(BF16), sub_keys.astype(BF16))
    gd = peer_gates(idx, gate)
    return peer_experts(x1, xn, gd, expert_u.T.astype(BF16), expert_v.astype(BF16), final_w.reshape(1, -1), final_norm,
                        rows_first)


def kernel(x_prompt, x_sample, rel_bias, norm1_w, w_in, conv_w, conv_b, A_log, dt_bias, D_skip, ssd_norm_w,
           w_ssd_out, w_attn_out, w_out, norm2_w, w_query, sub_keys, expert_u, expert_v, final_norm_w):
    nb_prompt = x_prompt.shape[0]
    x = jnp.concatenate([x_prompt, x_sample], axis=0)
    B, L, _ = x.shape
    x2d = x.reshape(B * L, D_MODEL)
    biases = [prep_attn_bias(rel_bias, gi, dil) for gi, (_, dil) in enumerate(DILATION_GROUPS)]
    for l in range(DEPTH):
        last = l == DEPTH - 1
        x2d = trunk_layer(x2d, B, L, biases, norm1_w[l], w_in[l], conv_w[l], conv_b[l], A_log[l], dt_bias[l],
                          D_skip[l], ssd_norm_w[l], w_ssd_out[l], w_attn_out[l], w_out[l], norm2_w[l], w_query[l],
                          sub_keys[l], expert_u[l], expert_v[l], final_norm_w, last,
                          nb_prompt * L if last else None)
    y_prompt, y_sample = x2d
    return y_prompt.reshape(nb_prompt, L, D_MODEL), y_sample.reshape(B - nb_prompt, L, D_MODEL)
```

```python
import functools

import numpy as np
import jax
import jax.numpy as jnp
from jax import lax
from jax.experimental import pallas as pl
from jax.experimental.pallas import tpu as pltpu

F32 = jnp.float32
BF16 = jnp.bfloat16

D_MODEL = 1024
DEPTH = 2
D_INNER = 2048
SSD_HEADS = 32
SSD_HEAD_DIM = 64
SSD_GROUPS = 8
SSD_STATE = 128
CHUNK = 128
D_CONV = 5
GN = SSD_GROUPS * SSD_STATE
CONV_DIM = D_INNER + 2 * GN
ATTN_HEAD_DIM = 64
HEADS_PER_GROUP = 8
DILATION_GROUPS = ((128, 1), (512, 4), (2048, 16))
ATTN_WIDTH = 1536
ATTN_OUT_WIDTH = 512
REL_BUCKETS = 32
REL_MAX_DISTANCE = 1024
PEER_HEADS = 8
PEER_N_KEYS = 128
PEER_EXPERTS = PEER_N_KEYS * PEER_N_KEYS
PEER_HALF = 128
PEER_TOPK = 16
RMS_EPS = 1e-6

OFF_XBC = D_INNER
OFF_DT = OFF_XBC + CONV_DIM
OFF_Q = OFF_DT + 2 * SSD_HEADS

OFF_K = OFF_Q + ATTN_WIDTH
OFF_V = OFF_K + ATTN_WIDTH
OFF_GATE = OFF_V + ATTN_WIDTH

PC_Z = 0
PC_GATE = D_INNER
PC_XBC = PC_GATE + 2 * D_MODEL
PC_Q = PC_XBC + CONV_DIM
PC_K = PC_Q + ATTN_WIDTH
PC_V = PC_K + ATTN_WIDTH
PROJ_COLS = PC_V + ATTN_WIDTH
DT_REP = 16
DTX_COLS = 128

ATT_SPAN = 64
ATT_TQ = 128
ATT_TK = 256
NEG = -1e30

VMEM_LIMIT = 56 * 1024 * 1024


def _cparams(n_axes):
    return pltpu.CompilerParams(
        dimension_semantics=("arbitrary",) * n_axes, vmem_limit_bytes=VMEM_LIMIT
    )


def _inproj_kernel(x_ref, nw_ref, w_ref, wdt_ref, o_ref, dt_ref, xn_ref):
    @pl.when(pl.program_id(1) == 0)
    def _():
        x = x_ref[...]
        ms = jnp.mean(x * x, axis=-1, keepdims=True)
        xn = ((x * lax.rsqrt(ms + RMS_EPS)) * nw_ref[...]).astype(BF16)
        xn_ref[...] = xn
        dt_ref[...] = jnp.dot(xn, wdt_ref[...], preferred_element_type=F32)

    o_ref[...] = jnp.dot(xn_ref[...], w_ref[...], preferred_element_type=F32).astype(BF16)


def in_proj(x2d, norm_w, w_main, w_dtx, tm=1024, tn=2560):
    T = x2d.shape[0]
    nc = w_main.shape[1]
    return pl.pallas_call(
        _inproj_kernel,
        grid=(T // tm, nc // tn),
        in_specs=[
            pl.BlockSpec((tm, D_MODEL), lambda i, j: (i, 0)),
            pl.BlockSpec((1, D_MODEL), lambda i, j: (0, 0)),
            pl.BlockSpec((D_MODEL, tn), lambda i, j: (0, j)),
            pl.BlockSpec((D_MODEL, DTX_COLS), lambda i, j: (0, 0)),
        ],
        out_specs=[
            pl.BlockSpec((tm, tn), lambda i, j: (i, j)),
            pl.BlockSpec((tm, DTX_COLS), lambda i, j: (i, 0)),
        ],
        out_shape=[
            jax.ShapeDtypeStruct((T, nc), BF16),
            jax.ShapeDtypeStruct((T, DTX_COLS), F32),
        ],
        scratch_shapes=[pltpu.VMEM((tm, D_MODEL), BF16)],
        compiler_params=_cparams(2),
        name="in_proj",
    )(x2d, norm_w, w_main, w_dtx)


SSD_HALO = 64


def _split3(x):
    hi = x.astype(BF16).astype(F32)
    r1 = x - hi
    mid = r1.astype(BF16).astype(F32)
    lo = (r1 - mid).astype(BF16).astype(F32)
    return hi, mid, lo


def _ssd_kernel(x_ref, b_ref, c_ref, dt_ref, cwx_ref, cwb_ref, cwc_ref, cbx_ref, cbb_ref, cbc_ref,
                dtb_ref, alog_ref, dsk_ref, y_ref,
                pad_ref, xs_ref, bt_ref, cm_ref, yacc_ref, st_ref, pk_ref, ng_ref):
    L = x_ref.shape[1]
    nch = L // CHUNK
    cw = jnp.concatenate([cwx_ref[...], cwb_ref[...], cwc_ref[...]], axis=1)
    cb = jnp.concatenate([cbx_ref[...], cbb_ref[...], cbc_ref[...]], axis=1)
    dsk = dsk_ref[0]

    zhalo = jnp.zeros((SSD_HALO, 512), BF16)
    pad_ref[0:SSD_HALO, :] = zhalo
    pad_ref[L + SSD_HALO:L + 2 * SSD_HALO, :] = zhalo

    def copy_body(c, carry):
        r = pl.multiple_of(c * CHUNK, CHUNK)
        dst = pl.ds(r + SSD_HALO, CHUNK)
        pad_ref[dst, 0:256] = x_ref[0, pl.ds(r, CHUNK), :]
        pad_ref[dst, 256:384] = b_ref[0, pl.ds(r, CHUNK), :]
        pad_ref[dst, 384:512] = c_ref[0, pl.ds(r, CHUNK), :]
        return carry

    lax.fori_loop(0, nch, copy_body, 0)

    taps = (-2, -1, 1, 2)
    srow = lax.broadcasted_iota(jnp.int32, (len(taps) * CHUNK, CHUNK + 2 * SSD_HALO), 0)
    scol = lax.broadcasted_iota(jnp.int32, (len(taps) * CHUNK, CHUNK + 2 * SSD_HALO), 1)
    soff = jnp.where(srow < CHUNK, taps[0], jnp.where(srow < 2 * CHUNK, taps[1], jnp.where(srow < 3 * CHUNK, taps[2], taps[3])))
    shift_mat = (scol == (srow & (CHUNK - 1)) + SSD_HALO + soff).astype(BF16)

    def conv_body(c, carry):
        r = pl.multiple_of(c * CHUNK, CHUNK)
        win = pad_ref[pl.ds(r, CHUNK + 2 * SSD_HALO), :]
        sh = jnp.dot(shift_mat, win, preferred_element_type=F32)
        acc = win[SSD_HALO:SSD_HALO + CHUNK, :].astype(F32) * cw[2:3, :] + cb
        for i, w in enumerate((0, 1, 3, 4)):
            acc = acc + sh[i * CHUNK:(i + 1) * CHUNK, :] * cw[w:w + 1, :]
        act = acc * jax.nn.sigmoid(acc)
        xs = act[:, 0:256]
        xs_ref[pl.ds(r, CHUNK), :] = xs
        yacc_ref[pl.ds(r, CHUNK), :] = xs * dsk
        bt_ref[c] = act[:, 256:384].T.astype(BF16)
        cm_ref[pl.ds(r, CHUNK), :] = act[:, 384:512].astype(BF16)
        return carry

    lax.fori_loop(0, nch, conv_body, 0, unroll=4)

    st_ref[...] = jnp.zeros_like(st_ref)

    row = lax.broadcasted_iota(jnp.int32, (CHUNK, CHUNK), 0)
    col = lax.broadcasted_iota(jnp.int32, (CHUNK, CHUNK), 1)
    lane_lt64 = col < 64
    a_neg = -jnp.exp(alog_ref[0])
    dtb = dtb_ref[0]
    erow = lax.broadcasted_iota(jnp.int32, (CHUNK, 2 * CHUNK), 0)
    ecol = lax.broadcasted_iota(jnp.int32, (CHUNK, 2 * CHUNK), 1)

    def selectors(d):
        rel = erow - (64 * d + DT_REP * lax.shift_right_logical(ecol, 6))
        sel_cs = ((rel >= 0) & (rel < 3)).astype(BF16)
        sel_dt = ((rel >= 6) & (rel < 9)).astype(BF16)
        return sel_cs, sel_dt

    sels = (selectors(0), selectors(1))
    tri3_pre = jnp.concatenate([(row <= col).astype(BF16)] * 3, axis=0)
    tri3_suf = jnp.concatenate([(row >= col).astype(BF16)] * 3, axis=0)
    r8 = lax.broadcasted_iota(jnp.int32, (8, CHUNK), 0)
    sel_pack = ((col == DT_REP * (row & 7) + lax.shift_right_logical(row, 3)) & (row < 72)).astype(BF16)
    head_rows = [((row >= DT_REP * k) & (row < DT_REP * k + 6)).astype(BF16) for k in range(8)]
    half_lo = lane_lt64.astype(BF16)
    half_hi = 1 - half_lo
    ones8 = jnp.ones((8, CHUNK), F32)

    raw = dt_ref[0, 0]
    dt_all = jnp.concatenate([jax.nn.softplus(raw[:, c * CHUNK:(c + 1) * CHUNK] + dtb) for c in range(nch)], axis=0)
    a_all = dt_all * jnp.concatenate([a_neg] * nch, axis=0)
    a3 = jnp.concatenate([t.astype(BF16) for t in _split3(a_all)], axis=1)
    rk = lax.broadcasted_iota(jnp.int32, (8 * nch, CHUNK), 0) & 7
    cs_all = jnp.where(rk < 4,
                       jnp.dot(a3, tri3_pre, preferred_element_type=F32),
                       jnp.dot(a3, tri3_suf, preferred_element_type=F32))
    c_parts = _split3(cs_all)
    d_parts = _split3(dt_all)
    for c in range(nch):
        sl = slice(8 * c, 8 * c + 8)
        stack = jnp.concatenate([p[sl] for p in c_parts] + [ones8] * 3 + [p[sl] for p in d_parts]
                                + [jnp.zeros((CHUNK - 72, CHUNK), F32)], axis=0)
        pk_ref[c] = jnp.dot(stack.T.astype(BF16), sel_pack, preferred_element_type=F32).astype(BF16)
        stack_n = jnp.concatenate([ones8] * 3 + [-p[sl] for p in c_parts]
                                  + [jnp.zeros((CHUNK - 48, CHUNK), F32)], axis=0)
        negs = jnp.dot(stack_n.T.astype(BF16), sel_pack, preferred_element_type=F32)
        negs_t = negs.T.astype(BF16)
        for d in range(2):
            ng_ref[c, d] = jnp.concatenate([negs_t * head_rows[4 * d + h] for h in range(4)], axis=1)

    def one_pass(d, c):
        r = pl.multiple_of(c * CHUNK, CHUNK)
        rows = pl.ds(r, CHUNK)
        mask = (row >= col) if d == 0 else (row <= col)
        last = CHUNK - 1 if d == 0 else 0
        packed = pk_ref[c]
        sel_cs, sel_dt = sels[d]
        cs2 = jnp.dot(packed, sel_cs, preferred_element_type=F32)
        dt2 = jnp.dot(packed, sel_dt, preferred_element_type=F32)
        diff = jnp.dot(packed, ng_ref[c, d], preferred_element_type=F32)
        tot2 = cs2[last:last + 1, :]
        ein2 = jnp.exp(cs2)
        dd2 = jnp.exp(tot2 - cs2) * dt2
        cd2 = jnp.exp(tot2)
        cmat = cm_ref[rows, :]
        btc = bt_ref[c]
        cbm = jnp.dot(cmat, btc, preferred_element_type=F32)
        xs = xs_ref[rows, :]
        xg = (xs * dt2).astype(BF16)
        xdec = (xs * dd2).astype(BF16)
        ydiag = []
        for p in range(2):
            cols = slice(128 * p, 128 * p + 128)
            gs = []
            for hh in range(2):
                h = 2 * p + hh
                seg = jnp.where(mask, jnp.exp(diff[:, 128 * h:128 * h + 128]), 0.0)
                gs.append((cbm * seg).astype(BF16))
            x2 = xg[:, cols]
            ydiag.append(jnp.dot(jnp.concatenate(gs, axis=1), jnp.concatenate([x2 * half_lo, x2 * half_hi], axis=0),
                                 preferred_element_type=F32))
        st = st_ref[d]
        yoff = jnp.dot(cmat, st.astype(BF16), preferred_element_type=F32) * ein2
        yacc_ref[rows, :] += jnp.concatenate(ydiag, axis=1) + yoff
        st_ref[d] = st * cd2 + jnp.dot(btc, xdec, preferred_element_type=F32)

    per_trip = 8

    def main_body(i, carry):
        for u in range(per_trip):
            one_pass(0, per_trip * i + u)
            one_pass(1, nch - 1 - per_trip * i - u)
        return carry

    lax.fori_loop(0, nch // per_trip, main_body, 0)

    def out_body(c, carry):
        rows = pl.ds(pl.multiple_of(c * CHUNK, CHUNK), CHUNK)
        y_ref[0, rows, :] = yacc_ref[rows, :].astype(BF16)
        return carry

    lax.fori_loop(0, nch, out_body, 0)


def ssd_mixer(proj3, dtx3, conv_w, conv_b, dtb_g, alog_g, dsk_g):
    B, L, _ = proj3.shape
    xb = PC_XBC // 256
    bb = (PC_XBC + D_INNER) // 128
    cbk = (PC_XBC + D_INNER + GN) // 128
    return pl.pallas_call(
        _ssd_kernel,
        grid=(B, SSD_GROUPS),
        in_specs=[
            pl.BlockSpec((1, L, 256), lambda b, g: (b, 0, xb + g)),
            pl.BlockSpec((1, L, 128), lambda b, g: (b, 0, bb + g)),
            pl.BlockSpec((1, L, 128), lambda b, g: (b, 0, cbk + g)),
            pl.BlockSpec((1, 1, 8, L), lambda b, g: (b, g, 0, 0)),
            pl.BlockSpec((D_CONV, 256), lambda b, g: (0, g)),
            pl.BlockSpec((D_CONV, 128), lambda b, g: (0, D_INNER // 128 + g)),
            pl.BlockSpec((D_CONV, 128), lambda b, g: (0, (D_INNER + GN) // 128 + g)),
            pl.BlockSpec((1, 256), lambda b, g: (0, g)),
            pl.BlockSpec((1, 128), lambda b, g: (0, D_INNER // 128 + g)),
            pl.BlockSpec((1, 128), lambda b, g: (0, (D_INNER + GN) // 128 + g)),
            pl.BlockSpec((1, 8, CHUNK), lambda b, g: (g, 0, 0)),
            pl.BlockSpec((1, 8, CHUNK), lambda b, g: (g, 0, 0)),
            pl.BlockSpec((1, 1, 256), lambda b, g: (g, 0, 0)),
        ],
        out_specs=pl.BlockSpec((1, L, 256), lambda b, g: (b, 0, g)),
        out_shape=jax.ShapeDtypeStruct((B, L, D_INNER), BF16),
        scratch_shapes=[
            pltpu.VMEM((L + 2 * SSD_HALO, 512), BF16),
            pltpu.VMEM((L, 256), F32),
            pltpu.VMEM((L // CHUNK, SSD_STATE, CHUNK), BF16),
            pltpu.VMEM((L, SSD_STATE), BF16),
            pltpu.VMEM((L, 256), F32),
            pltpu.VMEM((2, SSD_STATE, 256), F32),
            pltpu.VMEM((L // CHUNK, CHUNK, 128), BF16),
            pltpu.VMEM((L // CHUNK, 2, 128, 4 * CHUNK), BF16),
        ],
        compiler_params=_cparams(2),
        name="ssd_mixer",
    )(proj3, proj3, proj3, dtx3, conv_w, conv_w, conv_w, conv_b, conv_b, conv_b, dtb_g, alog_g, dsk_g)


def _attn_kernel(q_ref, k_ref, v_ref, bias_ref, o_ref, lse_ref, kp_ref, vp_ref, *stage, dil):
    L = q_ref.shape[1]
    M = L // dil
    nt = M // ATT_TQ
    width = q_ref.shape[2]
    nslab = width // 128
    zpad = jnp.zeros((ATT_SPAN, width), BF16)
    kp_ref[0:ATT_SPAN, :] = zpad
    kp_ref[M + ATT_SPAN:M + 2 * ATT_SPAN, :] = zpad
    vp_ref[0:ATT_SPAN, :] = zpad
    vp_ref[M + ATT_SPAN:M + 2 * ATT_SPAN, :] = zpad

    if dil > 1:
        qf_ref, kf_ref, vf_ref, of_ref, lf_ref = stage

        def stage_body(c, carry):
            rows = pl.ds(pl.multiple_of(c * ATT_TQ, ATT_TQ), ATT_TQ)
            for s in range(nslab):
                cols = slice(128 * s, 128 * s + 128)
                qf_ref[s, rows, :] = q_ref[0, rows, cols].astype(F32)
                kf_ref[s, rows, :] = k_ref[0, rows, cols].astype(F32)
                vf_ref[s, rows, :] = v_ref[0, rows, cols].astype(F32)
            return carry

        lax.fori_loop(0, L // ATT_TQ, stage_body, 0)
    else:
        qf_ref = kf_ref = vf_ref = of_ref = lf_ref = None

    def sub_rows(r, t):
        return pl.ds(r + t * (ATT_TQ * dil), ATT_TQ, stride=dil)

    def load_tile(ref, ref_f32, r, t):
        if dil == 1:
            return ref[0, pl.ds(pl.multiple_of(t * ATT_TQ, ATT_TQ), ATT_TQ), :]
        return jnp.concatenate([ref_f32[s, sub_rows(r, t), :] for s in range(nslab)], axis=1).astype(BF16)

    lane = lax.broadcasted_iota(jnp.int32, (ATT_TQ, 128), 1)
    lane_lt64 = lane < 64

    def residue_body(r, carry):
        def copy_body(t, c2):
            dst = pl.ds(pl.multiple_of(t * ATT_TQ, ATT_TQ) + ATT_SPAN, ATT_TQ)
            kp_ref[dst, :] = load_tile(k_ref, kf_ref, r, t)
            vp_ref[dst, :] = load_tile(v_ref, vf_ref, r, t)
            return c2

        lax.fori_loop(0, nt, copy_body, 0)

        def tile_body(t, c2):
            r0 = pl.multiple_of(t * ATT_TQ, ATT_TQ)
            q = load_tile(q_ref, qf_ref, r, t) * jnp.asarray(ATTN_HEAD_DIM ** -0.5, BF16)
            kw = kp_ref[pl.ds(r0, ATT_TK), :]
            vw = vp_ref[pl.ds(r0, ATT_TK), :]
            edge = jnp.where(t == 0, 1, 0) + jnp.where(t == nt - 1, 2, 0)
            lse_tile = jnp.zeros((ATT_TQ, 128), F32)
            outs = []
            for p in range(HEADS_PER_GROUP // 2):
                cols = slice(128 * p, 128 * p + 128)
                q2, k2, v2 = q[:, cols], kw[:, cols], vw[:, cols]
                pv, dens = [], []
                for hh in range(2):
                    h = 2 * p + hh
                    qm = jnp.where(lane_lt64 if hh == 0 else jnp.logical_not(lane_lt64), q2, jnp.zeros_like(q2))
                    s = lax.dot_general(qm, k2, (((1,), (1,)), ((), ())), preferred_element_type=F32)
                    s = s + bias_ref[edge, h]
                    m = jnp.max(s, axis=-1, keepdims=True)
                    pexp = jnp.exp(s - m)
                    den = jnp.sum(pexp, axis=-1, keepdims=True)
                    pv.append(jnp.dot(pexp.astype(BF16), v2, preferred_element_type=F32))
                    dens.append(den)
                    lse_tile = jnp.where(lane == h, m + jnp.log(den), lse_tile)
                outs.append(jnp.where(lane_lt64, pv[0], pv[1]) / jnp.where(lane_lt64, dens[0], dens[1]))
            if dil == 1:
                o_ref[0, pl.ds(r0, ATT_TQ), :] = jnp.concatenate(outs, axis=1).astype(BF16)
                lse_ref[0, pl.ds(r0, ATT_TQ), :] = lse_tile
            else:
                for p, o2 in enumerate(outs):
                    of_ref[p, sub_rows(r, t), :] = o2
                lf_ref[sub_rows(r, t), :] = lse_tile
            return c2

        lax.fori_loop(0, nt, tile_body, 0, unroll=2 if nt > 1 else 1)
        return carry

    if dil == 1:
        residue_body(0, 0)
    else:
        lax.fori_loop(0, dil, residue_body, 0, unroll=2 if nt == 1 else 1)

        def out_body(c, carry):
            rows = pl.ds(pl.multiple_of(c * ATT_TQ, ATT_TQ), ATT_TQ)
            o_ref[0, rows, :] = jnp.concatenate([of_ref[s, rows, :] for s in range(nslab)], axis=1).astype(BF16)
            lse_ref[0, rows, :] = lf_ref[rows, :]
            return carry

        lax.fori_loop(0, L // ATT_TQ, out_body, 0)


def dilated_attention(proj3, bias_g, gi, dil):
    B, L, _ = proj3.shape
    M = L // dil
    width = HEADS_PER_GROUP * ATTN_HEAD_DIM
    nslab = width // 128
    qb, kb, vb = PC_Q // width + gi, PC_K // width + gi, PC_V // width + gi
    scratch = [
        pltpu.VMEM((M + 2 * ATT_SPAN, width), BF16),
        pltpu.VMEM((M + 2 * ATT_SPAN, width), BF16),
    ]
    if dil > 1:
        scratch += [pltpu.VMEM((nslab, L, 128), F32)] * 4 + [pltpu.VMEM((L, 128), F32)]
    return pl.pallas_call(
        functools.partial(_attn_kernel, dil=dil),
        grid=(B,),
        in_specs=[
            pl.BlockSpec((1, L, width), lambda b: (b, 0, qb)),
            pl.BlockSpec((1, L, width), lambda b: (b, 0, kb)),
            pl.BlockSpec((1, L, width), lambda b: (b, 0, vb)),
            pl.BlockSpec((4, HEADS_PER_GROUP, ATT_TQ, ATT_TK), lambda b: (0, 0, 0, 0)),
        ],
        out_specs=[
            pl.BlockSpec((1, L, width), lambda b: (b, 0, 0)),
            pl.BlockSpec((1, L, 128), lambda b: (b, 0, 0)),
        ],
        out_shape=[
            jax.ShapeDtypeStruct((B, L, width), BF16),
            jax.ShapeDtypeStruct((B, L, 128), F32),
        ],
        scratch_shapes=scratch,
        compiler_params=_cparams(1),
        name=f"dilated_attention_d{dil}",
    )(proj3, proj3, proj3, bias_g)


def _mix_kernel(x_ref, y_ref, z_ref, g_ref, o0_ref, o1_ref, o2_ref, l0_ref, l1_ref, l2_ref,
                nw_ref, wssd_ref, watt_ref, wout_ref, out_ref):
    z = z_ref[...].astype(F32)
    yz = y_ref[...].astype(F32) * (z * jax.nn.sigmoid(z))
    ms = jnp.mean(yz * yz, axis=-1, keepdims=True)
    yn = ((yz * lax.rsqrt(ms + RMS_EPS)) * nw_ref[...]).astype(BF16)
    pa = jnp.dot(yn, wssd_ref[...], preferred_element_type=F32)

    ls = [l0_ref[...], l1_ref[...], l2_ref[...]]
    mx = jnp.maximum(jnp.maximum(ls[0], ls[1]), ls[2])
    es = [jnp.exp(l - mx) for l in ls]
    den = es[0] + es[1] + es[2]
    erow = lax.broadcasted_iota(jnp.int32, (128, ATTN_OUT_WIDTH), 0)
    ecol = lax.broadcasted_iota(jnp.int32, (128, ATTN_OUT_WIDTH), 1)
    expand = (erow == lax.shift_right_logical(ecol, 6)).astype(BF16)
    expand3 = jnp.concatenate([expand] * 3, axis=0)
    yatt = None
    for e, o_ref in zip(es, (o0_ref, o1_ref, o2_ref)):
        w3 = jnp.concatenate([t.astype(BF16) for t in _split3(e / den)], axis=1)
        wexp = jnp.dot(w3, expand3, preferred_element_type=F32)
        term = wexp * o_ref[...].astype(F32)
        yatt = term if yatt is None else yatt + term
    pb = jnp.dot(yatt.astype(BF16), watt_ref[...], preferred_element_type=F32)

    gates = jax.nn.sigmoid(g_ref[...].astype(F32))
    merged = gates[:, :D_MODEL] * pa + gates[:, D_MODEL:] * pb
    out_ref[...] = x_ref[...] + jnp.dot(merged.astype(BF16), wout_ref[...], preferred_element_type=F32)


def mix_project(x2d, yssd, proj, outs, lses, norm_w, w_ssd, w_att, w_out, tm=512):
    T = x2d.shape[0]
    row = lambda i: (i, 0)
    const = lambda i: (0, 0)
    return pl.pallas_call(
        _mix_kernel,
        grid=(T // tm,),
        in_specs=[
            pl.BlockSpec((tm, D_MODEL), row),
            pl.BlockSpec((tm, D_INNER), row),
            pl.BlockSpec((tm, D_INNER), lambda i: (i, PC_Z // D_INNER)),
            pl.BlockSpec((tm, 2 * D_MODEL), lambda i: (i, PC_GATE // (2 * D_MODEL))),
            pl.BlockSpec((tm, ATTN_OUT_WIDTH), row),
            pl.BlockSpec((tm, ATTN_OUT_WIDTH), row),
            pl.BlockSpec((tm, ATTN_OUT_WIDTH), row),
            pl.BlockSpec((tm, 128), row),
            pl.BlockSpec((tm, 128), row),
            pl.BlockSpec((tm, 128), row),
            pl.BlockSpec((1, D_INNER), const),
            pl.BlockSpec((D_INNER, D_MODEL), const),
            pl.BlockSpec((ATTN_OUT_WIDTH, D_MODEL), const),
            pl.BlockSpec((D_MODEL, D_MODEL), const),
        ],
        out_specs=pl.BlockSpec((tm, D_MODEL), row),
        out_shape=jax.ShapeDtypeStruct((T, D_MODEL), F32),
        compiler_params=_cparams(1),
        name="mix_project",
    )(x2d, yssd, proj, proj, outs[0], outs[1], outs[2], lses[0], lses[1], lses[2], norm_w, w_ssd, w_att, w_out)


ROUTE_LANES = 128
N_CAND = 80


def _extract_top_paired(vals, ids, n, big):
    half = vals.shape[0] // 2
    first = vals[:half] >= vals[half:]
    win_v = jnp.where(first, vals[:half], vals[half:])
    win_i = jnp.where(first, ids[:half], ids[half:])
    los_v = jnp.where(first, vals[half:], vals[:half])
    los_i = jnp.where(first, ids[half:], ids[:half])
    out_v, out_i = [], []
    for _ in range(n):
        m = jnp.max(win_v, axis=0, keepdims=True)
        pick = jnp.min(jnp.where(win_v == m, win_i, big), axis=0, keepdims=True)
        out_v.append(m)
        out_i.append(pick)
        hit = win_i == pick
        win_v = jnp.where(hit, los_v, win_v)
        win_i = jnp.where(hit, los_i, win_i)
        los_v = jnp.where(hit, -jnp.inf, los_v)
    return out_v, out_i


def _stack_rows(rows):
    n = len(rows)
    rid = lax.broadcasted_iota(jnp.int32, (n, rows[0].shape[1]), 0)
    out = jnp.broadcast_to(rows[0], rid.shape)
    for k in range(1, n):
        out = jnp.where(rid == k, rows[k], out)
    return out


def _peer_route_kernel(x_ref, nw_ref, wq_ref, keys_ref, xn_ref, idx_ref, g_ref, qt_ref, it_ref, gt_ref):
    tm = x_ref.shape[0]
    x = x_ref[...]
    ms = jnp.mean(x * x, axis=-1, keepdims=True)
    xn = ((x * lax.rsqrt(ms + RMS_EPS)) * nw_ref[...]).astype(BF16)
    xn_ref[...] = xn
    qt_ref[...] = lax.dot_general(wq_ref[...], xn, (((1,), (1,)), ((), ())),
                                  preferred_element_type=F32).astype(BF16)

    key_id = lax.broadcasted_iota(jnp.int32, (PEER_N_KEYS, ROUTE_LANES), 0).astype(F32)
    r80 = lax.broadcasted_iota(jnp.int32, (N_CAND, ROUTE_LANES), 0)
    rank_a = jnp.where(r80 < 16, 0, jnp.where(r80 < 72, 1 + lax.shift_right_logical(r80 - 16, 3), r80 - 64))
    rank_b = jnp.where(r80 < 16, r80, jnp.where(r80 < 72, (r80 - 16) & 7, 0))
    code = (rank_a * PEER_TOPK + rank_b).astype(F32)

    def route(h, lane0):
        lanes = slice(lane0, lane0 + ROUTE_LANES)
        sv, si = [], []
        for s in range(2):
            qhs = qt_ref[pl.ds(pl.multiple_of((2 * h + s) * PEER_HALF, PEER_HALF), PEER_HALF), lanes]
            sc = jnp.dot(keys_ref[s], qhs, preferred_element_type=F32)
            v, i = _extract_top_paired(sc, key_id, PEER_TOPK, float(PEER_N_KEYS))
            sv.append(v)
            si.append(i)
        sv1 = _stack_rows(sv[1])
        si1 = _stack_rows(si[1])
        cand = [sv[0][0] + sv1]
        eid = [si[0][0] * PEER_N_KEYS + si1]
        for a in range(1, 8):
            cand.append(sv[0][a] + sv1[0:8])
            eid.append(si[0][a] * PEER_N_KEYS + si1[0:8])
        cand.append(_stack_rows(sv[0][8:16]) + sv1[0:1])
        eid.append(_stack_rows(si[0][8:16]) * PEER_N_KEYS + si1[0:1])
        cand = jnp.concatenate(cand, axis=0)
        eid = jnp.concatenate(eid, axis=0)
        cv, cc = _extract_top_paired(cand, code * PEER_EXPERTS + eid, PEER_TOPK, float(256 * PEER_EXPERTS))
        picked = [c_k - jnp.floor(c_k * (1.0 / PEER_EXPERTS)) * PEER_EXPERTS for c_k in cc]
        cvs = _stack_rows(cv)
        ex = jnp.exp(cvs - cvs[0:1])
        gate = ex / jnp.sum(ex, axis=0, keepdims=True)
        rows = pl.ds(pl.multiple_of(h * PEER_TOPK, PEER_TOPK), PEER_TOPK)
        it_ref[rows, lanes] = _stack_rows(picked)
        gt_ref[rows, lanes] = gate

    def body(h, carry):
        for lane0 in range(0, tm, ROUTE_LANES):
            route(h, lane0)
        return carry

    lax.fori_loop(0, PEER_HEADS, body, 0, unroll=8)
    idx_ref[...] = it_ref[...].T.astype(jnp.int32)
    g_ref[...] = gt_ref[...].T


def peer_route(x2d, norm_w, wq_t, keys, tm=256):
    T = x2d.shape[0]
    npick = PEER_HEADS * PEER_TOPK
    return pl.pallas_call(
        _peer_route_kernel,
        grid=(T // tm,),
        in_specs=[
            pl.BlockSpec((tm, D_MODEL), lambda i: (i, 0)),
            pl.BlockSpec((1, D_MODEL), lambda i: (0, 0)),
            pl.BlockSpec((2 * PEER_HEADS * PEER_HALF, D_MODEL), lambda i: (0, 0)),
            pl.BlockSpec((2, PEER_N_KEYS, PEER_HALF), lambda i: (0, 0, 0)),
        ],
        out_specs=[
            pl.BlockSpec((tm, D_MODEL), lambda i: (i, 0)),
            pl.BlockSpec((tm, npick), lambda i: (i, 0)),
            pl.BlockSpec((tm, npick), lambda i: (i, 0)),
        ],
        out_shape=[
            jax.ShapeDtypeStruct((T, D_MODEL), BF16),
            jax.ShapeDtypeStruct((T, npick), jnp.int32),
            jax.ShapeDtypeStruct((T, npick), F32),
        ],
        scratch_shapes=[
            pltpu.VMEM((2 * PEER_HEADS * PEER_HALF, tm), BF16),
            pltpu.VMEM((npick, tm), F32),
            pltpu.VMEM((npick, tm), F32),
        ],
        compiler_params=_cparams(1),
        name="peer_route",
    )(x2d, norm_w, wq_t, keys)


GATE_PITCH = 136
GATE_UNROLL = 64


def _peer_gates_kernel(idx_ref, g_ref, gd_ref, g3_ref):
    tm = idx_ref.shape[0]
    rid = lax.broadcasted_iota(jnp.int32, (PEER_N_KEYS, 128), 0)

    def body(tb, carry):
        t0 = pl.multiple_of(tb * GATE_UNROLL, GATE_UNROLL)
        e = idx_ref[pl.ds(t0, GATE_UNROLL), :]
        g = g_ref[pl.ds(t0, GATE_UNROLL), :]
        i1 = lax.shift_right_logical(e, 7)
        i2 = e & (PEER_N_KEYS - 1)
        for u in range(GATE_UNROLL):
            a_t = jnp.where(rid == i1[u:u + 1], g[u:u + 1], 0.0).astype(BF16)
            b_t = jnp.where(rid == i2[u:u + 1], 1.0, 0.0).astype(BF16)
            gd = lax.dot_general(a_t, b_t, (((1,), (1,)), ((), ())), preferred_element_type=F32)
            g3_ref[pl.ds(t0 + u, PEER_N_KEYS, stride=GATE_PITCH), :] = gd
        return carry

    lax.fori_loop(0, tm // GATE_UNROLL, body, 0)
    for a in range(PEER_N_KEYS):
        gd_ref[:, a * PEER_N_KEYS:(a + 1) * PEER_N_KEYS] = g3_ref[a * GATE_PITCH:a * GATE_PITCH + tm, :].astype(BF16)


def peer_gates(idx, g, tm=128):
    T, npick = idx.shape
    return pl.pallas_call(
        _peer_gates_kernel,
        grid=(T // tm,),
        in_specs=[pl.BlockSpec((tm, npick), lambda i: (i, 0)), pl.BlockSpec((tm, npick), lambda i: (i, 0))],
        out_specs=pl.BlockSpec((tm, PEER_EXPERTS), lambda i: (i, 0)),
        out_shape=jax.ShapeDtypeStruct((T, PEER_EXPERTS), BF16),
        scratch_shapes=[pltpu.VMEM((PEER_N_KEYS * GATE_PITCH, PEER_N_KEYS), F32)],
        compiler_params=_cparams(1),
        name="peer_gates",
    )(idx, g)


def _peer_experts_kernel(x_ref, xn_ref, gd_ref, ut_ref, v_ref, fw_ref, *rest, final_norm, n_first):
    out_refs, acc_ref = rest[:-1], rest[-1]
    i = pl.program_id(0)
    j = pl.program_id(1)

    @pl.when(j == 0)
    def _():
        acc_ref[...] = jnp.zeros_like(acc_ref)

    z = jnp.dot(xn_ref[...], ut_ref[...], preferred_element_type=F32)
    hid = 0.5 * z * (1.0 + lax.erf(z * np.float32(np.sqrt(0.5))))
    w = (gd_ref[...].astype(F32) * hid).astype(BF16)
    acc_ref[...] += jnp.dot(w, v_ref[...], preferred_element_type=F32)

    @pl.when(j == pl.num_programs(1) - 1)
    def _():
        y = x_ref[...] + acc_ref[...]
        if final_norm:
            ms = jnp.mean(y * y, axis=-1, keepdims=True)
            y = (y * lax.rsqrt(ms + RMS_EPS)) * fw_ref[...]
        if n_first is None:
            out_refs[0][...] = y
        else:
            @pl.when(i < n_first)
            def _():
                out_refs[0][...] = y

            @pl.when(i >= n_first)
            def _():
                out_refs[1][...] = y


def peer_experts(x2d, xn, gd, u_t, v, final_w, final_norm, rows_first=None, tm=512, te=2048):
    T = x2d.shape[0]
    if rows_first is None:
        n_first = None
        out_specs = pl.BlockSpec((tm, D_MODEL), lambda i, j: (i, 0))
        out_shape = jax.ShapeDtypeStruct((T, D_MODEL), F32)
    else:
        n_first = rows_first // tm
        out_specs = [
            pl.BlockSpec((tm, D_MODEL), lambda i, j: (jnp.minimum(i, n_first - 1), 0)),
            pl.BlockSpec((tm, D_MODEL), lambda i, j: (jnp.maximum(i - n_first, 0), 0)),
        ]
        out_shape = [jax.ShapeDtypeStruct((rows_first, D_MODEL), F32),
                     jax.ShapeDtypeStruct((T - rows_first, D_MODEL), F32)]
    return pl.pallas_call(
        functools.partial(_peer_experts_kernel, final_norm=final_norm, n_first=n_first),
        grid=(T // tm, PEER_EXPERTS // te),
        in_specs=[
            pl.BlockSpec((tm, D_MODEL), lambda i, j: (i, 0)),
            pl.BlockSpec((tm, D_MODEL), lambda i, j: (i, 0)),
            pl.BlockSpec((tm, te), lambda i, j: (i, j)),
            pl.BlockSpec((D_MODEL, te), lambda i, j: (0, j)),
            pl.BlockSpec((te, D_MODEL), lambda i, j: (j, 0)),
            pl.BlockSpec((1, D_MODEL), lambda i, j: (0, 0)),
        ],
        out_specs=out_specs,
        out_shape=out_shape,
        scratch_shapes=[pltpu.VMEM((tm, D_MODEL), F32)],
        compiler_params=_cparams(2),
        name="peer_experts_final" if final_norm else "peer_experts",
    )(x2d, xn, gd, u_t, v, final_w)


def _t5_bucket(rel):
    half = REL_BUCKETS // 2
    max_exact = half // 2
    n = np.abs(rel)
    large = max_exact + (np.log(np.maximum(n, 1) / max_exact) / np.log(REL_MAX_DISTANCE / max_exact)
                         * (half - max_exact)).astype(np.int32)
    large = np.minimum(large, half - 1)
    return np.where(rel > 0, half, 0) + np.where(n < max_exact, n, large)


def prep_attn_bias(rel_bias, gi, dil):
    i = np.arange(ATT_TQ)[:, None]
    j = np.arange(ATT_TK)[None, :]
    rel = j - ATT_SPAN - i
    bucket = _t5_bucket(rel * dil)
    heads = rel_bias[:, gi * HEADS_PER_GROUP:(gi + 1) * HEADS_PER_GROUP].astype(F32)
    table = jnp.zeros((HEADS_PER_GROUP, ATT_TQ, ATT_TK), F32)
    for b in np.unique(bucket):
        table = table + jnp.where((bucket == b)[None], heads[b][:, None, None], 0.0)
    window = np.abs(rel) <= ATT_SPAN
    after_start = np.broadcast_to(j >= ATT_SPAN, window.shape)
    before_end = np.broadcast_to(j < ATT_SPAN + ATT_TQ, window.shape)
    keep = np.stack([window, window & after_start, window & before_end, window & after_start & before_end])
    return jnp.where(keep[:, None], table[None], NEG)


def _group_major(t):
    lead = t.shape[:-1]
    t = jnp.swapaxes(t.reshape(lead + (2, SSD_GROUPS, 4)), -3, -2)
    return t.reshape(lead + (SSD_GROUPS, 8))


def prep_w_in(w_in_l):
    w_main = jnp.concatenate(
        [w_in_l[:, :OFF_XBC], w_in_l[:, OFF_GATE:], w_in_l[:, OFF_XBC:OFF_DT], w_in_l[:, OFF_Q:OFF_GATE]], axis=1
    ).astype(BF16)
    w_dt = _group_major(w_in_l[:, OFF_DT:OFF_Q].astype(BF16)).reshape(D_MODEL, 2 * SSD_HEADS)
    return w_main, jnp.pad(w_dt, ((0, 0), (0, DTX_COLS - 2 * SSD_HEADS)))


def prep_ssd_params(dt_bias_l, a_log_l, d_skip_l):
    lanes = (SSD_GROUPS, 8, CHUNK)
    dtb_g = jnp.broadcast_to(_group_major(dt_bias_l.reshape(-1))[..., None], lanes)
    alog_g = jnp.broadcast_to(_group_major(a_log_l.reshape(-1))[..., None], lanes)
    dsk_g = jnp.repeat(d_skip_l, SSD_HEAD_DIM).reshape(SSD_GROUPS, 1, 256)
    return dtb_g, alog_g, dsk_g


def dt_by_group(dt2d, B, L):
    return jnp.transpose(dt2d[:, :2 * SSD_HEADS].reshape(B, L, SSD_GROUPS, 8), (0, 2, 3, 1))


def trunk_layer(x2d, B, L, biases, norm1_w, w_in, conv_w, conv_b, a_log, dt_bias, d_skip, ssd_norm_w,
                w_ssd_out, w_attn_out, w_out, norm2_w, w_query, sub_keys, expert_u, expert_v, final_w, final_norm,
                rows_first=None):
    w_main, w_dtx = prep_w_in(w_in)
    proj, dtx = in_proj(x2d, norm1_w.reshape(1, -1), w_main, w_dtx)
    proj3 = proj.reshape(B, L, PROJ_COLS)
    yssd = ssd_mixer(proj3, dt_by_group(dtx, B, L), conv_w, conv_b.reshape(1, -1),
                     *prep_ssd_params(dt_bias, a_log, d_skip))
    outs, lses = [], []
    for gi, (_, dil) in enumerate(DILATION_GROUPS):
        o, lse = dilated_attention(proj3, biases[gi], gi, dil)
        outs.append(o.reshape(B * L, ATTN_OUT_WIDTH))
        lses.append(lse.reshape(B * L, 128))
    x1 = mix_project(x2d, yssd.reshape(B * L, D_INNER), proj, outs, lses, ssd_norm_w.reshape(1, -1),
                     w_ssd_out.astype(BF16), w_attn_out.astype(BF16), w_out.astype(BF16))
    xn, idx, gate = peer_route(x1, norm2_w.reshape(1, -1), w_query.T.astype(BF16), sub_keys.astype(BF16))
    gd = peer_gates(idx, gate)
    return peer_experts(x1, xn, gd, expert_u.T.astype(BF16), expert_v.astype(BF16), final_w.reshape(1, -1), final_norm,
                        rows_first)


def kernel(x_prompt, x_sample, rel_bias, norm1_w, w_in, conv_w, conv_b, A_log, dt_bias, D_skip, ssd_norm_w,
           w_ssd_out, w_attn_out, w_out, norm2_w, w_query, sub_keys, expert_u, expert_v, final_norm_w):
    nb_prompt = x_prompt.shape[0]
    x = jnp.concatenate([x_prompt, x_sample], axis=0)
    B, L, _ = x.shape
    x2d = x.reshape(B * L, D_MODEL)
    biases = [prep_attn_bias(rel_bias, gi, dil) for gi, (_, dil) in enumerate(DILATION_GROUPS)]
    for l in range(DEPTH):
        last = l == DEPTH - 1
        x2d = trunk_layer(x2d, B, L, biases, norm1_w[l], w_in[l], conv_w[l], conv_b[l], A_log[l], dt_bias[l],
                          D_skip[l], ssd_norm_w[l], w_ssd_out[l], w_attn_out[l], w_out[l], norm2_w[l], w_query[l],
                          sub_keys[l], expert_u[l], expert_v[l], final_norm_w, last,
                          nb_prompt * L if last else None)
    y_prompt, y_sample = x2d
    return y_prompt.reshape(nb_prompt, L, D_MODEL), y_sample.reshape(B - nb_prompt, L, D_MODEL)
```

```python
import functools

import numpy as np
import jax
import jax.numpy as jnp
from jax import lax
from jax.experimental import pallas as pl
from jax.experimental.pallas import tpu as pltpu

F32 = jnp.float32
BF16 = jnp.bfloat16

D_MODEL = 1024
DEPTH = 2
D_INNER = 2048
SSD_HEADS = 32
SSD_HEAD_DIM = 64
SSD_GROUPS = 8
SSD_STATE = 128
CHUNK = 128
D_CONV = 5
GN = SSD_GROUPS * SSD_STATE
CONV_DIM = D_INNER + 2 * GN
ATTN_HEAD_DIM = 64
HEADS_PER_GROUP = 8
DILATION_GROUPS = ((128, 1), (512, 4), (2048, 16))
ATTN_WIDTH = 1536
ATTN_OUT_WIDTH = 512
REL_BUCKETS = 32
REL_MAX_DISTANCE = 1024
PEER_HEADS = 8
PEER_N_KEYS = 128
PEER_EXPERTS = PEER_N_KEYS * PEER_N_KEYS
PEER_HALF = 128
PEER_TOPK = 16
RMS_EPS = 1e-6

OFF_XBC = D_INNER
OFF_DT = OFF_XBC + CONV_DIM
OFF_Q = OFF_DT + 2 * SSD_HEADS

OFF_K = OFF_Q + ATTN_WIDTH
OFF_V = OFF_K + ATTN_WIDTH
OFF_GATE = OFF_V + ATTN_WIDTH

PC_Z = 0
PC_GATE = D_INNER
PC_XBC = PC_GATE + 2 * D_MODEL
PC_Q = PC_XBC + CONV_DIM
PC_K = PC_Q + ATTN_WIDTH
PC_V = PC_K + ATTN_WIDTH
PROJ_COLS = PC_V + ATTN_WIDTH
DT_REP = 16
DTX_COLS = 128

ATT_SPAN = 64
ATT_TQ = 128
ATT_TK = 256
NEG = -1e30

VMEM_LIMIT = 56 * 1024 * 1024


def _cparams(n_axes):
    return pltpu.CompilerParams(
        dimension_semantics=("arbitrary",) * n_axes, vmem_limit_bytes=VMEM_LIMIT
    )


def _row_parts(x_parts, tm):
    xa, xb = (x_parts[0], x_parts[-1])
    n_first = xa.shape[0] // tm
    total = xa.shape[0] + (xb.shape[0] if len(x_parts) == 2 else 0)
    return xa, xb, n_first, total


def _part_specs(tm, n_first, grid_rank):
    if grid_rank == 1:
        return [pl.BlockSpec((tm, D_MODEL), lambda i: (jnp.minimum(i, n_first - 1), 0)),
                pl.BlockSpec((tm, D_MODEL), lambda i: (jnp.maximum(i - n_first, 0), 0))]
    return [pl.BlockSpec((tm, D_MODEL), lambda i, j: (jnp.minimum(i, n_first - 1), 0)),
            pl.BlockSpec((tm, D_MODEL), lambda i, j: (jnp.maximum(i - n_first, 0), 0))]


def _inproj_kernel(xa_ref, xb_ref, nw_ref, w_ref, wdt_ref, o_ref, dt_ref, xn_ref, *, n_first):
    i = pl.program_id(0)
    first_col = pl.program_id(1) == 0

    def normalise(x_ref):
        x = x_ref[...]
        ms = jnp.mean(x * x, axis=-1, keepdims=True)
        xn = ((x * lax.rsqrt(ms + RMS_EPS)) * nw_ref[...]).astype(BF16)
        xn_ref[...] = xn
        dt_ref[...] = jnp.dot(xn, wdt_ref[...], preferred_element_type=F32)

    @pl.when(first_col & (i < n_first))
    def _():
        normalise(xa_ref)

    @pl.when(first_col & (i >= n_first))
    def _():
        normalise(xb_ref)

    o_ref[...] = jnp.dot(xn_ref[...], w_ref[...], preferred_element_type=F32).astype(BF16)


def in_proj(x_parts, norm_w, w_main, w_dtx, tm=1024, tn=2560):
    xa, xb, n_first, T = _row_parts(x_parts, tm)
    nc = w_main.shape[1]
    return pl.pallas_call(
        functools.partial(_inproj_kernel, n_first=n_first),
        grid=(T // tm, nc // tn),
        in_specs=_part_specs(tm, n_first, 2) + [
            pl.BlockSpec((1, D_MODEL), lambda i, j: (0, 0)),
            pl.BlockSpec((D_MODEL, tn), lambda i, j: (0, j)),
            pl.BlockSpec((D_MODEL, DTX_COLS), lambda i, j: (0, 0)),
        ],
        out_specs=[
            pl.BlockSpec((tm, tn), lambda i, j: (i, j)),
            pl.BlockSpec((tm, DTX_COLS), lambda i, j: (i, 0)),
        ],
        out_shape=[
            jax.ShapeDtypeStruct((T, nc), BF16),
            jax.ShapeDtypeStruct((T, DTX_COLS), F32),
        ],
        scratch_shapes=[pltpu.VMEM((tm, D_MODEL), BF16)],
        compiler_params=_cparams(2),
        name="in_proj",
    )(xa, xb, norm_w, w_main, w_dtx)


SSD_HALO = 64


def _split3(x):
    hi = x.astype(BF16).astype(F32)
    r1 = x - hi
    mid = r1.astype(BF16).astype(F32)
    lo = (r1 - mid).astype(BF16).astype(F32)
    return hi, mid, lo


def _ssd_kernel(x_ref, b_ref, c_ref, dt_ref, cwx_ref, cwb_ref, cwc_ref, cbx_ref, cbb_ref, cbc_ref,
                dtb_ref, alog_ref, dsk_ref, y_ref,
                pad_ref, xs_ref, bt_ref, cm_ref, yacc_ref, st_ref, pk_ref, ng_ref):
    L = x_ref.shape[1]
    nch = L // CHUNK
    cw = jnp.concatenate([cwx_ref[...], cwb_ref[...], cwc_ref[...]], axis=1)
    cb = jnp.concatenate([cbx_ref[...], cbb_ref[...], cbc_ref[...]], axis=1)
    dsk = dsk_ref[0]

    zhalo = jnp.zeros((SSD_HALO, 512), BF16)
    pad_ref[0:SSD_HALO, :] = zhalo
    pad_ref[L + SSD_HALO:L + 2 * SSD_HALO, :] = zhalo

    def copy_body(c, carry):
        r = pl.multiple_of(c * CHUNK, CHUNK)
        dst = pl.ds(r + SSD_HALO, CHUNK)
        pad_ref[dst, 0:256] = x_ref[0, pl.ds(r, CHUNK), :]
        pad_ref[dst, 256:384] = b_ref[0, pl.ds(r, CHUNK), :]
        pad_ref[dst, 384:512] = c_ref[0, pl.ds(r, CHUNK), :]
        return carry

    lax.fori_loop(0, nch, copy_body, 0)

    taps = (-2, -1, 1, 2)
    srow = lax.broadcasted_iota(jnp.int32, (len(taps) * CHUNK, CHUNK + 2 * SSD_HALO), 0)
    scol = lax.broadcasted_iota(jnp.int32, (len(taps) * CHUNK, CHUNK + 2 * SSD_HALO), 1)
    soff = jnp.where(srow < CHUNK, taps[0], jnp.where(srow < 2 * CHUNK, taps[1], jnp.where(srow < 3 * CHUNK, taps[2], taps[3])))
    shift_mat = (scol == (srow & (CHUNK - 1)) + SSD_HALO + soff).astype(BF16)

    def conv_body(c, carry):
        r = pl.multiple_of(c * CHUNK, CHUNK)
        win = pad_ref[pl.ds(r, CHUNK + 2 * SSD_HALO), :]
        sh = jnp.dot(shift_mat, win, preferred_element_type=F32)
        acc = win[SSD_HALO:SSD_HALO + CHUNK, :].astype(F32) * cw[2:3, :] + cb
        for i, w in enumerate((0, 1, 3, 4)):
            acc = acc + sh[i * CHUNK:(i + 1) * CHUNK, :] * cw[w:w + 1, :]
        act = acc * jax.nn.sigmoid(acc)
        xs = act[:, 0:256]
        xs_ref[pl.ds(r, CHUNK), :] = xs
        yacc_ref[pl.ds(r, CHUNK), :] = xs * dsk
        bt_ref[c] = act[:, 256:384].T.astype(BF16)
        cm_ref[pl.ds(r, CHUNK), :] = act[:, 384:512].astype(BF16)
        return carry

    lax.fori_loop(0, nch, conv_body, 0, unroll=4)

    st_ref[...] = jnp.zeros_like(st_ref)

    row = lax.broadcasted_iota(jnp.int32, (CHUNK, CHUNK), 0)
    col = lax.broadcasted_iota(jnp.int32, (CHUNK, CHUNK), 1)
    lane_lt64 = col < 64
    a_neg = -jnp.exp(alog_ref[0])
    dtb = dtb_ref[0]
    erow = lax.broadcasted_iota(jnp.int32, (CHUNK, 2 * CHUNK), 0)
    ecol = lax.broadcasted_iota(jnp.int32, (CHUNK, 2 * CHUNK), 1)

    def selectors(d):
        rel = erow - (64 * d + DT_REP * lax.shift_right_logical(ecol, 6))
        sel_cs = ((rel >= 0) & (rel < 3)).astype(BF16)
        sel_dt = ((rel >= 6) & (rel < 9)).astype(BF16)
        return sel_cs, sel_dt

    sels = (selectors(0), selectors(1))
    tri3_pre = jnp.concatenate([(row <= col).astype(BF16)] * 3, axis=0)
    tri3_suf = jnp.concatenate([(row >= col).astype(BF16)] * 3, axis=0)
    r8 = lax.broadcasted_iota(jnp.int32, (8, CHUNK), 0)
    sel_pack = ((col == DT_REP * (row & 7) + lax.shift_right_logical(row, 3)) & (row < 72)).astype(BF16)
    head_rows = [((row >= DT_REP * k) & (row < DT_REP * k + 6)).astype(BF16) for k in range(8)]
    half_lo = lane_lt64.astype(BF16)
    half_hi = 1 - half_lo
    ones8 = jnp.ones((8, CHUNK), F32)

    raw = dt_ref[0, 0]
    dt_all = jnp.concatenate([jax.nn.softplus(raw[:, c * CHUNK:(c + 1) * CHUNK] + dtb) for c in range(nch)], axis=0)
    a_all = dt_all * jnp.concatenate([a_neg] * nch, axis=0)
    a3 = jnp.concatenate([t.astype(BF16) for t in _split3(a_all)], axis=1)
    rk = lax.broadcasted_iota(jnp.int32, (8 * nch, CHUNK), 0) & 7
    cs_all = jnp.where(rk < 4,
                       jnp.dot(a3, tri3_pre, preferred_element_type=F32),
                       jnp.dot(a3, tri3_suf, preferred_element_type=F32))
    c_parts = _split3(cs_all)
    d_parts = _split3(dt_all)
    for c in range(nch):
        sl = slice(8 * c, 8 * c + 8)
        stack = jnp.concatenate([p[sl] for p in c_parts] + [ones8] * 3 + [p[sl] for p in d_parts]
                                + [jnp.zeros((CHUNK - 72, CHUNK), F32)], axis=0)
        pk_ref[c] = jnp.dot(stack.T.astype(BF16), sel_pack, preferred_element_type=F32).astype(BF16)
        stack_n = jnp.concatenate([ones8] * 3 + [-p[sl] for p in c_parts]
                                  + [jnp.zeros((CHUNK - 48, CHUNK), F32)], axis=0)
        negs = jnp.dot(stack_n.T.astype(BF16), sel_pack, preferred_element_type=F32)
        negs_t = negs.T.astype(BF16)
        for d in range(2):
            ng_ref[c, d] = jnp.concatenate([negs_t * head_rows[4 * d + h] for h in range(4)], axis=1)

    def one_pass(d, c):
        r = pl.multiple_of(c * CHUNK, CHUNK)
        rows = pl.ds(r, CHUNK)
        mask = (row >= col) if d == 0 else (row <= col)
        last = CHUNK - 1 if d == 0 else 0
        packed = pk_ref[c]
        sel_cs, sel_dt = sels[d]
        cs2 = jnp.dot(packed, sel_cs, preferred_element_type=F32)
        dt2 = jnp.dot(packed, sel_dt, preferred_element_type=F32)
        diff = jnp.dot(packed, ng_ref[c, d], preferred_element_type=F32)
        tot2 = cs2[last:last + 1, :]
        ein2 = jnp.exp(cs2)
        dd2 = jnp.exp(tot2 - cs2) * dt2
        cd2 = jnp.exp(tot2)
        cmat = cm_ref[rows, :]
        btc = bt_ref[c]
        cbm = jnp.dot(cmat, btc, preferred_element_type=F32)
        xs = xs_ref[rows, :]
        xg = (xs * dt2).astype(BF16)
        xdec = (xs * dd2).astype(BF16)
        ydiag = []
        for p in range(2):
            cols = slice(128 * p, 128 * p + 128)
            gs = []
            for hh in range(2):
                h = 2 * p + hh
                seg = jnp.where(mask, jnp.exp(diff[:, 128 * h:128 * h + 128]), 0.0)
                gs.append((cbm * seg).astype(BF16))
            x2 = xg[:, cols]
            ydiag.append(jnp.dot(jnp.concatenate(gs, axis=1), jnp.concatenate([x2 * half_lo, x2 * half_hi], axis=0),
                                 preferred_element_type=F32))
        st = st_ref[d]
        yoff = jnp.dot(cmat, st.astype(BF16), preferred_element_type=F32) * ein2
        yacc_ref[rows, :] += jnp.concatenate(ydiag, axis=1) + yoff
        st_ref[d] = st * cd2 + jnp.dot(btc, xdec, preferred_element_type=F32)

    per_trip = 8

    def main_body(i, carry):
        for u in range(per_trip):
            one_pass(0, per_trip * i + u)
            one_pass(1, nch - 1 - per_trip * i - u)
        return carry

    lax.fori_loop(0, nch // per_trip, main_body, 0)

    def out_body(c, carry):
        rows = pl.ds(pl.multiple_of(c * CHUNK, CHUNK), CHUNK)
        y_ref[0, rows, :] = yacc_ref[rows, :].astype(BF16)
        return carry

    lax.fori_loop(0, nch, out_body, 0)


def ssd_mixer(proj3, dtx3, conv_w, conv_b, dtb_g, alog_g, dsk_g):
    B, L, _ = proj3.shape
    xb = PC_XBC // 256
    bb = (PC_XBC + D_INNER) // 128
    cbk = (PC_XBC + D_INNER + GN) // 128
    return pl.pallas_call(
        _ssd_kernel,
        grid=(B, SSD_GROUPS),
        in_specs=[
            pl.BlockSpec((1, L, 256), lambda b, g: (b, 0, xb + g)),
            pl.BlockSpec((1, L, 128), lambda b, g: (b, 0, bb + g)),
            pl.BlockSpec((1, L, 128), lambda b, g: (b, 0, cbk + g)),
            pl.BlockSpec((1, 1, 8, L), lambda b, g: (b, g, 0, 0)),
            pl.BlockSpec((D_CONV, 256), lambda b, g: (0, g)),
            pl.BlockSpec((D_CONV, 128), lambda b, g: (0, D_INNER // 128 + g)),
            pl.BlockSpec((D_CONV, 128), lambda b, g: (0, (D_INNER + GN) // 128 + g)),
            pl.BlockSpec((1, 256), lambda b, g: (0, g)),
            pl.BlockSpec((1, 128), lambda b, g: (0, D_INNER // 128 + g)),
            pl.BlockSpec((1, 128), lambda b, g: (0, (D_INNER + GN) // 128 + g)),
            pl.BlockSpec((1, 8, CHUNK), lambda b, g: (g, 0, 0)),
            pl.BlockSpec((1, 8, CHUNK), lambda b, g: (g, 0, 0)),
            pl.BlockSpec((1, 1, 256), lambda b, g: (g, 0, 0)),
        ],
        out_specs=pl.BlockSpec((1, L, 256), lambda b, g: (b, 0, g)),
        out_shape=jax.ShapeDtypeStruct((B, L, D_INNER), BF16),
        scratch_shapes=[
            pltpu.VMEM((L + 2 * SSD_HALO, 512), BF16),
            pltpu.VMEM((L, 256), F32),
            pltpu.VMEM((L // CHUNK, SSD_STATE, CHUNK), BF16),
            pltpu.VMEM((L, SSD_STATE), BF16),
            pltpu.VMEM((L, 256), F32),
            pltpu.VMEM((2, SSD_STATE, 256), F32),
            pltpu.VMEM((L // CHUNK, CHUNK, 128), BF16),
            pltpu.VMEM((L // CHUNK, 2, 128, 4 * CHUNK), BF16),
        ],
        compiler_params=_cparams(2),
        name="ssd_mixer",
    )(proj3, proj3, proj3, dtx3, conv_w, conv_w, conv_w, conv_b, conv_b, conv_b, dtb_g, alog_g, dsk_g)


def _attn_kernel(q_ref, k_ref, v_ref, bias_ref, o_ref, lse_ref, kp_ref, vp_ref, *stage, dil):
    L = q_ref.shape[1]
    M = L // dil
    nt = M // ATT_TQ
    width = q_ref.shape[2]
    nslab = width // 128
    zpad = jnp.zeros((ATT_SPAN, width), BF16)
    kp_ref[0:ATT_SPAN, :] = zpad
    kp_ref[M + ATT_SPAN:M + 2 * ATT_SPAN, :] = zpad
    vp_ref[0:ATT_SPAN, :] = zpad
    vp_ref[M + ATT_SPAN:M + 2 * ATT_SPAN, :] = zpad

    if dil > 1:
        qf_ref, kf_ref, vf_ref, of_ref, lf_ref = stage

        def stage_body(c, carry):
            rows = pl.ds(pl.multiple_of(c * ATT_TQ, ATT_TQ), ATT_TQ)
            for s in range(nslab):
                cols = slice(128 * s, 128 * s + 128)
                qf_ref[s, rows, :] = q_ref[0, rows, cols].astype(F32)
                kf_ref[s, rows, :] = k_ref[0, rows, cols].astype(F32)
                vf_ref[s, rows, :] = v_ref[0, rows, cols].astype(F32)
            return carry

        lax.fori_loop(0, L // ATT_TQ, stage_body, 0)
    else:
        qf_ref = kf_ref = vf_ref = of_ref = lf_ref = None

    def sub_rows(r, t):
        return pl.ds(r + t * (ATT_TQ * dil), ATT_TQ, stride=dil)

    def load_tile(ref, ref_f32, r, t):
        if dil == 1:
            return ref[0, pl.ds(pl.multiple_of(t * ATT_TQ, ATT_TQ), ATT_TQ), :]
        return jnp.concatenate([ref_f32[s, sub_rows(r, t), :] for s in range(nslab)], axis=1).astype(BF16)

    lane = lax.broadcasted_iota(jnp.int32, (ATT_TQ, 128), 1)
    lane_lt64 = lane < 64

    def residue_body(r, carry):
        def copy_body(t, c2):
            dst = pl.ds(pl.multiple_of(t * ATT_TQ, ATT_TQ) + ATT_SPAN, ATT_TQ)
            kp_ref[dst, :] = load_tile(k_ref, kf_ref, r, t)
            vp_ref[dst, :] = load_tile(v_ref, vf_ref, r, t)
            return c2

        lax.fori_loop(0, nt, copy_body, 0)

        def tile_body(t, c2):
            r0 = pl.multiple_of(t * ATT_TQ, ATT_TQ)
            q = load_tile(q_ref, qf_ref, r, t) * jnp.asarray(ATTN_HEAD_DIM ** -0.5, BF16)
            kw = kp_ref[pl.ds(r0, ATT_TK), :]
            vw = vp_ref[pl.ds(r0, ATT_TK), :]
            edge = jnp.where(t == 0, 1, 0) + jnp.where(t == nt - 1, 2, 0)
            lse_tile = jnp.zeros((ATT_TQ, 128), F32)
            outs = []
            for p in range(HEADS_PER_GROUP // 2):
                cols = slice(128 * p, 128 * p + 128)
                q2, k2, v2 = q[:, cols], kw[:, cols], vw[:, cols]
                pv, dens = [], []
                for hh in range(2):
                    h = 2 * p + hh
                    qm = jnp.where(lane_lt64 if hh == 0 else jnp.logical_not(lane_lt64), q2, jnp.zeros_like(q2))
                    s = lax.dot_general(qm, k2, (((1,), (1,)), ((), ())), preferred_element_type=F32)
                    s = s + bias_ref[edge, h]
                    m = jnp.max(s, axis=-1, keepdims=True)
                    pexp = jnp.exp(s - m)
                    den = jnp.sum(pexp, axis=-1, keepdims=True)
                    pv.append(jnp.dot(pexp.astype(BF16), v2, preferred_element_type=F32))
                    dens.append(den)
                    lse_tile = jnp.where(lane == h, m + jnp.log(den), lse_tile)
                outs.append(jnp.where(lane_lt64, pv[0], pv[1]) / jnp.where(lane_lt64, dens[0], dens[1]))
            if dil == 1:
                o_ref[0, pl.ds(r0, ATT_TQ), :] = jnp.concatenate(outs, axis=1).astype(BF16)
                lse_ref[0, pl.ds(r0, ATT_TQ), :] = lse_tile
            else:
                for p, o2 in enumerate(outs):
                    of_ref[p, sub_rows(r, t), :] = o2
                lf_ref[sub_rows(r, t), :] = lse_tile
            return c2

        lax.fori_loop(0, nt, tile_body, 0, unroll=2 if nt > 1 else 1)
        return carry

    if dil == 1:
        residue_body(0, 0)
    else:
        lax.fori_loop(0, dil, residue_body, 0, unroll=2 if nt == 1 else 1)

        def out_body(c, carry):
            rows = pl.ds(pl.multiple_of(c * ATT_TQ, ATT_TQ), ATT_TQ)
            o_ref[0, rows, :] = jnp.concatenate([of_ref[s, rows, :] for s in range(nslab)], axis=1).astype(BF16)
            lse_ref[0, rows, :] = lf_ref[rows, :]
            return carry

        lax.fori_loop(0, L // ATT_TQ, out_body, 0)


def dilated_attention(proj3, bias_g, gi, dil):
    B, L, _ = proj3.shape
    M = L // dil
    width = HEADS_PER_GROUP * ATTN_HEAD_DIM
    nslab = width // 128
    qb, kb, vb = PC_Q // width + gi, PC_K // width + gi, PC_V // width + gi
    scratch = [
        pltpu.VMEM((M + 2 * ATT_SPAN, width), BF16),
        pltpu.VMEM((M + 2 * ATT_SPAN, width), BF16),
    ]
    if dil > 1:
        scratch += [pltpu.VMEM((nslab, L, 128), F32)] * 4 + [pltpu.VMEM((L, 128), F32)]
    return pl.pallas_call(
        functools.partial(_attn_kernel, dil=dil),
        grid=(B,),
        in_specs=[
            pl.BlockSpec((1, L, width), lambda b: (b, 0, qb)),
            pl.BlockSpec((1, L, width), lambda b: (b, 0, kb)),
            pl.BlockSpec((1, L, width), lambda b: (b, 0, vb)),
            pl.BlockSpec((4, HEADS_PER_GROUP, ATT_TQ, ATT_TK), lambda b: (0, 0, 0, 0)),
        ],
        out_specs=[
            pl.BlockSpec((1, L, width), lambda b: (b, 0, 0)),
            pl.BlockSpec((1, L, 128), lambda b: (b, 0, 0)),
        ],
        out_shape=[
            jax.ShapeDtypeStruct((B, L, width), BF16),
            jax.ShapeDtypeStruct((B, L, 128), F32),
        ],
        scratch_shapes=scratch,
        compiler_params=_cparams(1),
        name=f"dilated_attention_d{dil}",
    )(proj3, proj3, proj3, bias_g)


def _mix_kernel(xa_ref, xb_ref, y_ref, z_ref, g_ref, o0_ref, o1_ref, o2_ref, l0_ref, l1_ref, l2_ref,
                nw_ref, wssd_ref, watt_ref, wout_ref, out_ref, *, n_first):
    z = z_ref[...].astype(F32)
    yz = y_ref[...].astype(F32) * (z * jax.nn.sigmoid(z))
    ms = jnp.mean(yz * yz, axis=-1, keepdims=True)
    yn = ((yz * lax.rsqrt(ms + RMS_EPS)) * nw_ref[...]).astype(BF16)
    pa = jnp.dot(yn, wssd_ref[...], preferred_element_type=F32)

    ls = [l0_ref[...], l1_ref[...], l2_ref[...]]
    mx = jnp.maximum(jnp.maximum(ls[0], ls[1]), ls[2])
    es = [jnp.exp(l - mx) for l in ls]
    den = es[0] + es[1] + es[2]
    erow = lax.broadcasted_iota(jnp.int32, (128, ATTN_OUT_WIDTH), 0)
    ecol = lax.broadcasted_iota(jnp.int32, (128, ATTN_OUT_WIDTH), 1)
    expand = (erow == lax.shift_right_logical(ecol, 6)).astype(BF16)
    expand3 = jnp.concatenate([expand] * 3, axis=0)
    yatt = None
    for e, o_ref in zip(es, (o0_ref, o1_ref, o2_ref)):
        w3 = jnp.concatenate([t.astype(BF16) for t in _split3(e / den)], axis=1)
        wexp = jnp.dot(w3, expand3, preferred_element_type=F32)
        term = wexp * o_ref[...].astype(F32)
        yatt = term if yatt is None else yatt + term
    pb = jnp.dot(yatt.astype(BF16), watt_ref[...], preferred_element_type=F32)

    gates = jax.nn.sigmoid(g_ref[...].astype(F32))
    merged = gates[:, :D_MODEL] * pa + gates[:, D_MODEL:] * pb
    delta = jnp.dot(merged.astype(BF16), wout_ref[...], preferred_element_type=F32)
    i = pl.program_id(0)

    @pl.when(i < n_first)
    def _():
        out_ref[...] = xa_ref[...] + delta

    @pl.when(i >= n_first)
    def _():
        out_ref[...] = xb_ref[...] + delta


def mix_project(x_parts, yssd, proj, outs, lses, norm_w, w_ssd, w_att, w_out, tm=512):
    xa, xb, n_first, T = _row_parts(x_parts, tm)
    row = lambda i: (i, 0)
    const = lambda i: (0, 0)
    return pl.pallas_call(
        functools.partial(_mix_kernel, n_first=n_first),
        grid=(T // tm,),
        in_specs=_part_specs(tm, n_first, 1) + [
            pl.BlockSpec((tm, D_INNER), row),
            pl.BlockSpec((tm, D_INNER), lambda i: (i, PC_Z // D_INNER)),
            pl.BlockSpec((tm, 2 * D_MODEL), lambda i: (i, PC_GATE // (2 * D_MODEL))),
            pl.BlockSpec((tm, ATTN_OUT_WIDTH), row),
            pl.BlockSpec((tm, ATTN_OUT_WIDTH), row),
            pl.BlockSpec((tm, ATTN_OUT_WIDTH), row),
            pl.BlockSpec((tm, 128), row),
            pl.BlockSpec((tm, 128), row),
            pl.BlockSpec((tm, 128), row),
            pl.BlockSpec((1, D_INNER), const),
            pl.BlockSpec((D_INNER, D_MODEL), const),
            pl.BlockSpec((ATTN_OUT_WIDTH, D_MODEL), const),
            pl.BlockSpec((D_MODEL, D_MODEL), const),
        ],
        out_specs=pl.BlockSpec((tm, D_MODEL), row),
        out_shape=jax.ShapeDtypeStruct((T, D_MODEL), F32),
        compiler_params=_cparams(1),
        name="mix_project",
    )(xa, xb, yssd, proj, proj, outs[0], outs[1], outs[2], lses[0], lses[1], lses[2], norm_w, w_ssd, w_att, w_out)


ROUTE_LANES = 128
N_CAND = 80


def _extract_top_paired(vals, ids, n, big):
    half = vals.shape[0] // 2
    first = vals[:half] >= vals[half:]
    win_v = jnp.where(first, vals[:half], vals[half:])
    win_i = jnp.where(first, ids[:half], ids[half:])
    los_v = jnp.where(first, vals[half:], vals[:half])
    los_i = jnp.where(first, ids[half:], ids[:half])
    out_v, out_i = [], []
    for _ in range(n):
        m = jnp.max(win_v, axis=0, keepdims=True)
        pick = jnp.min(jnp.where(win_v == m, win_i, big), axis=0, keepdims=True)
        out_v.append(m)
        out_i.append(pick)
        hit = win_i == pick
        win_v = jnp.where(hit, los_v, win_v)
        win_i = jnp.where(hit, los_i, win_i)
        los_v = jnp.where(hit, -jnp.inf, los_v)
    return out_v, out_i


def _stack_rows(rows):
    n = len(rows)
    rid = lax.broadcasted_iota(jnp.int32, (n, rows[0].shape[1]), 0)
    out = jnp.broadcast_to(rows[0], rid.shape)
    for k in range(1, n):
        out = jnp.where(rid == k, rows[k], out)
    return out


def _peer_route_kernel(x_ref, nw_ref, wq_ref, keys_ref, xn_ref, idx_ref, g_ref, qt_ref, it_ref, gt_ref):
    tm = x_ref.shape[0]
    x = x_ref[...]
    ms = jnp.mean(x * x, axis=-1, keepdims=True)
    xn = ((x * lax.rsqrt(ms + RMS_EPS)) * nw_ref[...]).astype(BF16)
    xn_ref[...] = xn
    qt_ref[...] = lax.dot_general(wq_ref[...], xn, (((1,), (1,)), ((), ())),
                                  preferred_element_type=F32).astype(BF16)

    key_id = lax.broadcasted_iota(jnp.int32, (PEER_N_KEYS, ROUTE_LANES), 0).astype(F32)
    r80 = lax.broadcasted_iota(jnp.int32, (N_CAND, ROUTE_LANES), 0)
    rank_a = jnp.where(r80 < 16, 0, jnp.where(r80 < 72, 1 + lax.shift_right_logical(r80 - 16, 3), r80 - 64))
    rank_b = jnp.where(r80 < 16, r80, jnp.where(r80 < 72, (r80 - 16) & 7, 0))
    code = (rank_a * PEER_TOPK + rank_b).astype(F32)

    def route(h, lane0):
        lanes = slice(lane0, lane0 + ROUTE_LANES)
        sv, si = [], []
        for s in range(2):
            qhs = qt_ref[pl.ds(pl.multiple_of((2 * h + s) * PEER_HALF, PEER_HALF), PEER_HALF), lanes]
            sc = jnp.dot(keys_ref[s], qhs, preferred_element_type=F32)
            v, i = _extract_top_paired(sc, key_id, PEER_TOPK, float(PEER_N_KEYS))
            sv.append(v)
            si.append(i)
        sv1 = _stack_rows(sv[1])
        si1 = _stack_rows(si[1])
        cand = [sv[0][0] + sv1]
        eid = [si[0][0] * PEER_N_KEYS + si1]
        for a in range(1, 8):
            cand.append(sv[0][a] + sv1[0:8])
            eid.append(si[0][a] * PEER_N_KEYS + si1[0:8])
        cand.append(_stack_rows(sv[0][8:16]) + sv1[0:1])
        eid.append(_stack_rows(si[0][8:16]) * PEER_N_KEYS + si1[0:1])
        cand = jnp.concatenate(cand, axis=0)
        eid = jnp.concatenate(eid, axis=0)
        cv, cc = _extract_top_paired(cand, code * PEER_EXPERTS + eid, PEER_TOPK, float(256 * PEER_EXPERTS))
        picked = [c_k - jnp.floor(c_k * (1.0 / PEER_EXPERTS)) * PEER_EXPERTS for c_k in cc]
        cvs = _stack_rows(cv)
        ex = jnp.exp(cvs - cvs[0:1])
        gate = ex / jnp.sum(ex, axis=0, keepdims=True)
        rows = pl.ds(pl.multiple_of(h * PEER_TOPK, PEER_TOPK), PEER_TOPK)
        it_ref[rows, lanes] = _stack_rows(picked)
        gt_ref[rows, lanes] = gate

    def body(h, carry):
        for lane0 in range(0, tm, ROUTE_LANES):
            route(h, lane0)
        return carry

    lax.fori_loop(0, PEER_HEADS, body, 0, unroll=8)
    idx_ref[...] = it_ref[...].T.astype(jnp.int32)
    g_ref[...] = gt_ref[...].T


def peer_route(x2d, norm_w, wq_t, keys, tm=256):
    T = x2d.shape[0]
    npick = PEER_HEADS * PEER_TOPK
    return pl.pallas_call(
        _peer_route_kernel,
        grid=(T // tm,),
        in_specs=[
            pl.BlockSpec((tm, D_MODEL), lambda i: (i, 0)),
            pl.BlockSpec((1, D_MODEL), lambda i: (0, 0)),
            pl.BlockSpec((2 * PEER_HEADS * PEER_HALF, D_MODEL), lambda i: (0, 0)),
            pl.BlockSpec((2, PEER_N_KEYS, PEER_HALF), lambda i: (0, 0, 0)),
        ],
        out_specs=[
            pl.BlockSpec((tm, D_MODEL), lambda i: (i, 0)),
            pl.BlockSpec((tm, npick), lambda i: (i, 0)),
            pl.BlockSpec((tm, npick), lambda i: (i, 0)),
        ],
        out_shape=[
            jax.ShapeDtypeStruct((T, D_MODEL), BF16),
            jax.ShapeDtypeStruct((T, npick), jnp.int32),
            jax.ShapeDtypeStruct((T, npick), F32),
        ],
        scratch_shapes=[
            pltpu.VMEM((2 * PEER_HEADS * PEER_HALF, tm), BF16),
            pltpu.VMEM((npick, tm), F32),
            pltpu.VMEM((npick, tm), F32),
        ],
        compiler_params=_cparams(1),
        name="peer_route",
    )(x2d, norm_w, wq_t, keys)


GATE_PITCH = 136
GATE_UNROLL = 64


def _peer_gates_kernel(idx_ref, g_ref, gd_ref, g3_ref):
    tm = idx_ref.shape[0]
    rid = lax.broadcasted_iota(jnp.int32, (PEER_N_KEYS, 128), 0)

    def body(tb, carry):
        t0 = pl.multiple_of(tb * GATE_UNROLL, GATE_UNROLL)
        e = idx_ref[pl.ds(t0, GATE_UNROLL), :]
        g = g_ref[pl.ds(t0, GATE_UNROLL), :]
        i1 = lax.shift_right_logical(e, 7)
        i2 = e & (PEER_N_KEYS - 1)
        for u in range(GATE_UNROLL):
            a_t = jnp.where(rid == i1[u:u + 1], g[u:u + 1], 0.0).astype(BF16)
            b_t = jnp.where(rid == i2[u:u + 1], 1.0, 0.0).astype(BF16)
            gd = lax.dot_general(a_t, b_t, (((1,), (1,)), ((), ())), preferred_element_type=F32)
            g3_ref[pl.ds(t0 + u, PEER_N_KEYS, stride=GATE_PITCH), :] = gd
        return carry

    lax.fori_loop(0, tm // GATE_UNROLL, body, 0)
    for a in range(PEER_N_KEYS):
        gd_ref[:, a * PEER_N_KEYS:(a + 1) * PEER_N_KEYS] = g3_ref[a * GATE_PITCH:a * GATE_PITCH + tm, :].astype(BF16)


def peer_gates(idx, g, tm=128):
    T, npick = idx.shape
    return pl.pallas_call(
        _peer_gates_kernel,
        grid=(T // tm,),
        in_specs=[pl.BlockSpec((tm, npick), lambda i: (i, 0)), pl.BlockSpec((tm, npick), lambda i: (i, 0))],
        out_specs=pl.BlockSpec((tm, PEER_EXPERTS), lambda i: (i, 0)),
        out_shape=jax.ShapeDtypeStruct((T, PEER_EXPERTS), BF16),
        scratch_shapes=[pltpu.VMEM((PEER_N_KEYS * GATE_PITCH, PEER_N_KEYS), F32)],
        compiler_params=_cparams(1),
        name="peer_gates",
    )(idx, g)


def _peer_experts_kernel(x_ref, xn_ref, gd_ref, ut_ref, v_ref, fw_ref, *rest, final_norm, n_first):
    out_refs, acc_ref = rest[:-1], rest[-1]
    i = pl.program_id(0)
    j = pl.program_id(1)

    @pl.when(j == 0)
    def _():
        acc_ref[...] = jnp.zeros_like(acc_ref)

    z = jnp.dot(xn_ref[...], ut_ref[...], preferred_element_type=F32)
    hid = 0.5 * z * (1.0 + lax.erf(z * np.float32(np.sqrt(0.5))))
    w = (gd_ref[...].astype(F32) * hid).astype(BF16)
    acc_ref[...] += jnp.dot(w, v_ref[...], preferred_element_type=F32)

    @pl.when(j == pl.num_programs(1) - 1)
    def _():
        y = x_ref[...] + acc_ref[...]
        if final_norm:
            ms = jnp.mean(y * y, axis=-1, keepdims=True)
            y = (y * lax.rsqrt(ms + RMS_EPS)) * fw_ref[...]
        if n_first is None:
            out_refs[0][...] = y
        else:
            @pl.when(i < n_first)
            def _():
                out_refs[0][...] = y

            @pl.when(i >= n_first)
            def _():
                out_refs[1][...] = y


def peer_experts(x2d, xn, gd, u_t, v, final_w, final_norm, rows_first=None, tm=512, te=2048):
    T = x2d.shape[0]
    if rows_first is None:
        n_first = None
        out_specs = pl.BlockSpec((tm, D_MODEL), lambda i, j: (i, 0))
        out_shape = jax.ShapeDtypeStruct((T, D_MODEL), F32)
    else:
        n_first = rows_first // tm
        out_specs = [
            pl.BlockSpec((tm, D_MODEL), lambda i, j: (jnp.minimum(i, n_first - 1), 0)),
            pl.BlockSpec((tm, D_MODEL), lambda i, j: (jnp.maximum(i - n_first, 0), 0)),
        ]
        out_shape = [jax.ShapeDtypeStruct((rows_first, D_MODEL), F32),
                     jax.ShapeDtypeStruct((T - rows_first, D_MODEL), F32)]
    return pl.pallas_call(
        functools.partial(_peer_experts_kernel, final_norm=final_norm, n_first=n_first),
        grid=(T // tm, PEER_EXPERTS // te),
        in_specs=[
            pl.BlockSpec((tm, D_MODEL), lambda i, j: (i, 0)),
            pl.BlockSpec((tm, D_MODEL), lambda i, j: (i, 0)),
            pl.BlockSpec((tm, te), lambda i, j: (i, j)),
            pl.BlockSpec((D_MODEL, te), lambda i, j: (0, j)),
            pl.BlockSpec((te, D_MODEL), lambda i, j: (j, 0)),
            pl.BlockSpec((1, D_MODEL), lambda i, j: (0, 0)),
        ],
        out_specs=out_specs,
        out_shape=out_shape,
        scratch_shapes=[pltpu.VMEM((tm, D_MODEL), F32)],
        compiler_params=_cparams(2),
        name="peer_experts_final" if final_norm else "peer_experts",
    )(x2d, xn, gd, u_t, v, final_w)


def _t5_bucket(rel):
    half = REL_BUCKETS // 2
    max_exact = half // 2
    n = np.abs(rel)
    large = max_exact + (np.log(np.maximum(n, 1) / max_exact) / np.log(REL_MAX_DISTANCE / max_exact)
                         * (half - max_exact)).astype(np.int32)
    large = np.minimum(large, half - 1)
    return np.where(rel > 0, half, 0) + np.where(n < max_exact, n, large)


def prep_attn_bias(rel_bias, gi, dil):
    i = np.arange(ATT_TQ)[:, None]
    j = np.arange(ATT_TK)[None, :]
    rel = j - ATT_SPAN - i
    bucket = _t5_bucket(rel * dil)
    heads = rel_bias[:, gi * HEADS_PER_GROUP:(gi + 1) * HEADS_PER_GROUP].astype(F32)
    table = jnp.zeros((HEADS_PER_GROUP, ATT_TQ, ATT_TK), F32)
    for b in np.unique(bucket):
        table = table + jnp.where((bucket == b)[None], heads[b][:, None, None], 0.0)
    window = np.abs(rel) <= ATT_SPAN
    after_start = np.broadcast_to(j >= ATT_SPAN, window.shape)
    before_end = np.broadcast_to(j < ATT_SPAN + ATT_TQ, window.shape)
    keep = np.stack([window, window & after_start, window & before_end, window & after_start & before_end])
    return jnp.where(keep[:, None], table[None], NEG)


def _group_major(t):
    lead = t.shape[:-1]
    t = jnp.swapaxes(t.reshape(lead + (2, SSD_GROUPS, 4)), -3, -2)
    return t.reshape(lead + (SSD_GROUPS, 8))


def prep_w_in(w_in_l):
    w_main = jnp.concatenate(
        [w_in_l[:, :OFF_XBC], w_in_l[:, OFF_GATE:], w_in_l[:, OFF_XBC:OFF_DT], w_in_l[:, OFF_Q:OFF_GATE]], axis=1
    ).astype(BF16)
    w_dt = _group_major(w_in_l[:, OFF_DT:OFF_Q].astype(BF16)).reshape(D_MODEL, 2 * SSD_HEADS)
    return w_main, jnp.pad(w_dt, ((0, 0), (0, DTX_COLS - 2 * SSD_HEADS)))


def prep_ssd_params(dt_bias_l, a_log_l, d_skip_l):
    lanes = (SSD_GROUPS, 8, CHUNK)
    dtb_g = jnp.broadcast_to(_group_major(dt_bias_l.reshape(-1))[..., None], lanes)
    alog_g = jnp.broadcast_to(_group_major(a_log_l.reshape(-1))[..., None], lanes)
    dsk_g = jnp.repeat(d_skip_l, SSD_HEAD_DIM).reshape(SSD_GROUPS, 1, 256)
    return dtb_g, alog_g, dsk_g


def dt_by_group(dt2d, B, L):
    return jnp.transpose(dt2d[:, :2 * SSD_HEADS].reshape(B, L, SSD_GROUPS, 8), (0, 2, 3, 1))


def trunk_layer(x_parts, B, L, biases, norm1_w, w_in, conv_w, conv_b, a_log, dt_bias, d_skip, ssd_norm_w,
                w_ssd_out, w_attn_out, w_out, norm2_w, w_query, sub_keys, expert_u, expert_v, final_w, final_norm,
                rows_first=None):
    w_main, w_dtx = prep_w_in(w_in)
    proj, dtx = in_proj(x_parts, norm1_w.reshape(1, -1), w_main, w_dtx)
    proj3 = proj.reshape(B, L, PROJ_COLS)
    yssd = ssd_mixer(proj3, dt_by_group(dtx, B, L), conv_w, conv_b.reshape(1, -1),
                     *prep_ssd_params(dt_bias, a_log, d_skip))
    outs, lses = [], []
    for gi, (_, dil) in enumerate(DILATION_GROUPS):
        o, lse = dilated_attention(proj3, biases[gi], gi, dil)
        outs.append(o.reshape(B * L, ATTN_OUT_WIDTH))
        lses.append(lse.reshape(B * L, 128))
    x1 = mix_project(x_parts, yssd.reshape(B * L, D_INNER), proj, outs, lses, ssd_norm_w.reshape(1, -1),
                     w_ssd_out.astype(BF16), w_attn_out.astype(BF16), w_out.astype(BF16))
    xn, idx, gate = peer_route(x1, norm2_w.reshape(1, -1), w_query.T.astype(BF16), sub_keys.astype(BF16))
    gd = peer_gates(idx, gate)
    return peer_experts(x1, xn, gd, expert_u.T.astype(BF16), expert_v.astype(BF16), final_w.reshape(1, -1), final_norm,
                        rows_first)


def kernel(x_prompt, x_sample, rel_bias, norm1_w, w_in, conv_w, conv_b, A_log, dt_bias, D_skip, ssd_norm_w,
           w_ssd_out, w_attn_out, w_out, norm2_w, w_query, sub_keys, expert_u, expert_v, final_norm_w):
    nb_prompt = x_prompt.shape[0]
    L = x_prompt.shape[1]
    B = nb_prompt + x_sample.shape[0]
    x_parts = (x_prompt.reshape(nb_prompt * L, D_MODEL), x_sample.reshape((B - nb_prompt) * L, D_MODEL))
    biases = [prep_attn_bias(rel_bias, gi, dil) for gi, (_, dil) in enumerate(DILATION_GROUPS)]
    for l in range(DEPTH):
        last = l == DEPTH - 1
        out = trunk_layer(x_parts, B, L, biases, norm1_w[l], w_in[l], conv_w[l], conv_b[l], A_log[l], dt_bias[l],
                          D_skip[l], ssd_norm_w[l], w_ssd_out[l], w_attn_out[l], w_out[l], norm2_w[l], w_query[l],
                          sub_keys[l], expert_u[l], expert_v[l], final_norm_w, last,
                          nb_prompt * L if last else None)
        x_parts = out if last else (out,)
    y_prompt, y_sample = x_parts
    return y_prompt.reshape(nb_prompt, L, D_MODEL), y_sample.reshape(B - nb_prompt, L, D_MODEL)
```
